```python
import math
import jax, jax.numpy as jnp
from jax import lax
import numpy as np

D_MODEL = 1024
BATCH = 4
SEQ = 8192
DEPTH = 2

GRID_W = 64
CTX_LEN = 256
Q_BLOCK = 128
ROPE_BASE = 10000.0
LN_EPS = 1e-6
RMS_EPS = 1e-6
DN_ALPHA = (2 * DEPTH) ** 0.25
DN_BETA = (8 * DEPTH) ** -0.25
FOURIER_GROUPS = 4
FOURIER_GROUP_DIM = D_MODEL // 8
FOURIER_WIDTH = FOURIER_GROUPS * FOURIER_GROUP_DIM
MLA_HEADS = 8
MLA_Q_LORA = D_MODEL // 4
MLA_KV_LORA = D_MODEL // 4
MLA_NOPE = 64
MLA_ROPE = 32
MLA_V = 64
MLA_SCALE = (MLA_NOPE + MLA_ROPE) ** -0.5
MLA_IN_WIDTH = FOURIER_WIDTH + MLA_Q_LORA + MLA_KV_LORA + MLA_ROPE
DIFF_HEADS = 8
DIFF_HEAD_DIM = 64
DIFF_QK_WIDTH = DIFF_HEADS * 2 * DIFF_HEAD_DIM
DIFF_V_WIDTH = DIFF_HEADS * 2 * DIFF_HEAD_DIM
DIFF_SCALE = DIFF_HEAD_DIM ** -0.5
FF_HIDDEN = int(math.ceil(8 * D_MODEL / 3 / 256)) * 256

kernel_name = "hybrid_fourier_mla_diffattn_prefix_dit"


def layer_norm(x, g=None, b=None):
    xf = x.astype(jnp.float32)
    mu = jnp.mean(xf, axis=-1, keepdims=True)
    var = jnp.mean(jnp.square(xf - mu), axis=-1, keepdims=True)
    y = (xf - mu) * lax.rsqrt(var + LN_EPS)
    if g is not None:
        y = y * g.astype(jnp.float32) + b.astype(jnp.float32)
    return y.astype(x.dtype)


def rms_norm(x, g):
    xf = x.astype(jnp.float32)
    y = xf * lax.rsqrt(jnp.mean(jnp.square(xf), axis=-1, keepdims=True) + RMS_EPS) * g.astype(jnp.float32)
    return y.astype(x.dtype)


def axial_rope(x, rows, cols):
    quarter = x.shape[-1] // 4
    inv = 1.0 / (ROPE_BASE ** (jnp.arange(quarter, dtype=jnp.float32) / quarter))
    bshape = (x.shape[1],) + (1,) * (x.ndim - 3) + (quarter,)
    xf = x.astype(jnp.float32)
    outs = []
    for pos, xh in ((rows, xf[..., :2 * quarter]), (cols, xf[..., 2 * quarter:])):
        ang = (pos.astype(jnp.float32)[:, None] * inv).reshape(bshape)
        cos, sin = jnp.cos(ang), jnp.sin(ang)
        x1, x2 = xh[..., :quarter], xh[..., quarter:]
        outs += [x1 * cos - x2 * sin, x1 * sin + x2 * cos]
    return jnp.concatenate(outs, axis=-1).astype(x.dtype)


def to_blocks(a):
    b, s = a.shape[0], a.shape[1]
    return jnp.moveaxis(a.reshape((b, s // Q_BLOCK, Q_BLOCK) + a.shape[2:]), 1, 0)


def from_blocks(o):
    nb, b, blk = o.shape[0], o.shape[1], o.shape[2]
    return jnp.moveaxis(o, 0, 1).reshape((b, nb * blk) + o.shape[3:])


def fourier_mix(u):
    b, n, _ = u.shape
    z = u.astype(jnp.float32).reshape(b, n, FOURIER_GROUPS, FOURIER_GROUP_DIM)
    z = jnp.fft.fft2(z, axes=(1, 3), norm='ortho').real
    return z.reshape(b, n, FOURIER_WIDTH).astype(u.dtype)


def mla_attend(qn, qr, kn, kr, v):
    s = jnp.einsum('bqhd,bkhd->bhqk', qn, kn) + jnp.einsum('bqhr,bkr->bhqk', qr, kr)
    p = jax.nn.softmax(s.astype(jnp.float32) * MLA_SCALE, axis=-1).astype(v.dtype)
    return jnp.einsum('bhqk,bkhd->bqhd', p, v)


def diff_attend(q, k, v, lam):
    s = jnp.einsum('bqhcd,bkhcd->bhcqk', q, k).astype(jnp.float32) * DIFF_SCALE
    p = jax.nn.softmax(s, axis=-1)
    a = (p[:, :, 0] - lam * p[:, :, 1]).astype(v.dtype)
    return jnp.einsum('bhqk,bkhe->bqhe', a, v)


def swiglu(h, w_gate, w_up, w_down):
    return (jax.nn.silu(h @ w_gate) * (h @ w_up)) @ w_down


def mixer_fourier_mla(h, hc, rows, cols, w_in, q_norm, w_uq, kv_norm, w_ukv, w_out, need_ctx):
    def project(z):
        b, n, _ = z.shape
        u = z @ w_in
        o1 = FOURIER_WIDTH
        o2 = o1 + MLA_Q_LORA
        o3 = o2 + MLA_KV_LORA
        f, cq, ckv, kr = u[..., :o1], u[..., o1:o2], u[..., o2:o3], u[..., o3:]
        q = (rms_norm(cq, q_norm) @ w_uq).reshape(b, n, MLA_HEADS, MLA_NOPE + MLA_ROPE)
        kv = (rms_norm(ckv, kv_norm) @ w_ukv).reshape(b, n, MLA_HEADS, MLA_NOPE + MLA_V)
        return f, q[..., :MLA_NOPE], q[..., MLA_NOPE:], kv[..., :MLA_NOPE], kv[..., MLA_NOPE:], kr

    b, s, _ = h.shape
    f, qn, qr, kn, v, kr = project(h)
    qr = axial_rope(qr, rows, cols)
    kr = axial_rope(kr, rows, cols)
    fc, qnc, qrc, knc, vc, krc = project(hc)
    kn_all = jnp.concatenate([knc, kn], axis=1)
    kr_all = jnp.concatenate([krc, kr], axis=1)
    v_all = jnp.concatenate([vc, v], axis=1)
    att = from_blocks(lax.map(lambda qb: mla_attend(qb[0], qb[1], kn_all, kr_all, v_all),
                              (to_blocks(qn), to_blocks(qr))))
    y = jnp.concatenate([fourier_mix(f), att.reshape(b, s, MLA_HEADS * MLA_V)], axis=-1) @ w_out
    yc = None
    if need_ctx:
        attc = mla_attend(qnc, qrc, knc, krc, vc)
        yc = jnp.concatenate([fourier_mix(fc), attc.reshape(b, hc.shape[1], MLA_HEADS * MLA_V)], axis=-1) @ w_out
    return y, yc


def mixer_diff(h, hc, rows, cols, w_in, lambda_q1, lambda_k1, lambda_q2, lambda_k2, subln, w_out,
               lambda_init, need_ctx):
    def project(z):
        b, n, _ = z.shape
        u = z @ w_in
        q = u[..., :DIFF_QK_WIDTH].reshape(b, n, DIFF_HEADS, 2, DIFF_HEAD_DIM)
        k = u[..., DIFF_QK_WIDTH:2 * DIFF_QK_WIDTH].reshape(b, n, DIFF_HEADS, 2, DIFF_HEAD_DIM)
        v = u[..., 2 * DIFF_QK_WIDTH:].reshape(b, n, DIFF_HEADS, 2 * DIFF_HEAD_DIM)
        return q, k, v

    def finish(o):
        b, n = o.shape[0], o.shape[1]
        o = rms_norm(o, subln) * (1.0 - lambda_init)
        return o.reshape(b, n, DIFF_V_WIDTH) @ w_out

    f32 = jnp.float32
    lam = (jnp.exp(jnp.sum(lambda_q1.astype(f32) * lambda_k1.astype(f32)))
           - jnp.exp(jnp.sum(lambda_q2.astype(f32) * lambda_k2.astype(f32))) + lambda_init)
    q, k, v = project(h)
    q = axial_rope(q, rows, cols)
    k = axial_rope(k, rows, cols)
    qc, kc, vc = project(hc)
    k_all = jnp.concatenate([kc, k], axis=1)
    v_all = jnp.concatenate([vc, v], axis=1)
    att = from_blocks(lax.map(lambda qb: diff_attend(qb, k_all, v_all, lam), to_blocks(q)))
    y = finish(att)
    yc = finish(diff_attend(qc, kc, vc, lam)) if need_ctx else None
    return y, yc


def setup_inputs(seed: int = 0) -> dict:
    key = jax.random.key(seed)
    ks = iter(jax.random.split(key, 48))

    def nrm(shape, scale):
        return jax.random.normal(next(ks), shape, jnp.float32) * scale

    def gain(n):
        return 1.0 + nrm((n,), 0.02)

    d = D_MODEL
    inp = {}
    inp['x'] = nrm((BATCH, SEQ, d), 1.0)
    inp['c'] = nrm((BATCH, d), 1.0)
    inp['ctx'] = nrm((BATCH, CTX_LEN, d), 1.0)
    inp['c_ctx'] = nrm((d,), 1.0)
    inp['l0_w_mod'] = nrm((d, 6 * d), 0.5 * d ** -0.5)
    inp['l0_b_mod'] = nrm((6 * d,), 0.02)
    inp['l0_w_in'] = nrm((d, MLA_IN_WIDTH), d ** -0.5)
    inp['l0_q_norm'] = gain(MLA_Q_LORA)
    inp['l0_w_uq'] = nrm((MLA_Q_LORA, MLA_HEADS * (MLA_NOPE + MLA_ROPE)), MLA_Q_LORA ** -0.5)
    inp['l0_kv_norm'] = gain(MLA_KV_LORA)
    inp['l0_w_ukv'] = nrm((MLA_KV_LORA, MLA_HEADS * (MLA_NOPE + MLA_V)), MLA_KV_LORA ** -0.5)
    inp['l0_w_out'] = nrm((FOURIER_WIDTH + MLA_HEADS * MLA_V, d), DN_BETA * (FOURIER_WIDTH + MLA_HEADS * MLA_V) ** -0.5)
    inp['l0_ln1_g'] = gain(d)
    inp['l0_ln1_b'] = nrm((d,), 0.02)
    inp['l0_w_gate'] = nrm((d, FF_HIDDEN), d ** -0.5)
    inp['l0_w_up'] = nrm((d, FF_HIDDEN), d ** -0.5)
    inp['l0_w_down'] = nrm((FF_HIDDEN, d), DN_BETA * FF_HIDDEN ** -0.5)
    inp['l0_ln2_g'] = gain(d)
    inp['l0_ln2_b'] = nrm((d,), 0.02)
    inp['l1_w_mod'] = nrm((d, 6 * d), 0.5 * d ** -0.5)
    inp['l1_b_mod'] = nrm((6 * d,), 0.02)
    inp['l1_w_in'] = nrm((d, 2 * DIFF_QK_WIDTH + DIFF_V_WIDTH), d ** -0.5)
    inp['l1_lambda_q1'] = nrm((DIFF_HEAD_DIM,), 0.1)
    inp['l1_lambda_k1'] = nrm((DIFF_HEAD_DIM,), 0.1)
    inp['l1_lambda_q2'] = nrm((DIFF_HEAD_DIM,), 0.1)
    inp['l1_lambda_k2'] = nrm((DIFF_HEAD_DIM,), 0.1)
    inp['l1_subln'] = gain(2 * DIFF_HEAD_DIM)
    inp['l1_w_out'] = nrm((DIFF_V_WIDTH, d), DN_BETA * DIFF_V_WIDTH ** -0.5)
    inp['l1_ln1_g'] = gain(d)
    inp['l1_ln1_b'] = nrm((d,), 0.02)
    inp['l1_w_gate'] = nrm((d, FF_HIDDEN), d ** -0.5)
    inp['l1_w_up'] = nrm((d, FF_HIDDEN), d ** -0.5)
    inp['l1_w_down'] = nrm((FF_HIDDEN, d), DN_BETA * FF_HIDDEN ** -0.5)
    inp['l1_ln2_g'] = gain(d)
    inp['l1_ln2_b'] = nrm((d,), 0.02)
    return inp


def reference(x, c, ctx, c_ctx,
              l0_w_mod, l0_b_mod, l0_w_in, l0_q_norm, l0_w_uq, l0_kv_norm, l0_w_ukv, l0_w_out,
              l0_ln1_g, l0_ln1_b, l0_w_gate, l0_w_up, l0_w_down, l0_ln2_g, l0_ln2_b,
              l1_w_mod, l1_b_mod, l1_w_in, l1_lambda_q1, l1_lambda_k1, l1_lambda_q2, l1_lambda_k2,
              l1_subln, l1_w_out, l1_ln1_g, l1_ln1_b, l1_w_gate, l1_w_up, l1_w_down, l1_ln2_g, l1_ln2_b):
    s = x.shape[1]
    ROWS = s // GRID_W
    rows = jnp.repeat(jnp.arange(ROWS, dtype=jnp.int32), GRID_W)
    cols = jnp.tile(jnp.arange(GRID_W, dtype=jnp.int32), ROWS)

    layers = [
        dict(w_mod=l0_w_mod, b_mod=l0_b_mod, ln1_g=l0_ln1_g, ln1_b=l0_ln1_b, w_gate=l0_w_gate,
             w_up=l0_w_up, w_down=l0_w_down, ln2_g=l0_ln2_g, ln2_b=l0_ln2_b,
             mix=(l0_w_in, l0_q_norm, l0_w_uq, l0_kv_norm, l0_w_ukv, l0_w_out)),
        dict(w_mod=l1_w_mod, b_mod=l1_b_mod, ln1_g=l1_ln1_g, ln1_b=l1_ln1_b, w_gate=l1_w_gate,
             w_up=l1_w_up, w_down=l1_w_down, ln2_g=l1_ln2_g, ln2_b=l1_ln2_b,
             mix=(l1_w_in, l1_lambda_q1, l1_lambda_k1, l1_lambda_q2, l1_lambda_k2, l1_subln, l1_w_out)),
    ]

    xc = ctx
    for i in range(DEPTH):
        p = layers[i]
        need_ctx = i < DEPTH - 1
        sh1, sc1, g1, sh2, sc2, g2 = jnp.split(jax.nn.silu(c)[:, None, :] @ p['w_mod'] + p['b_mod'], 6, axis=-1)
        csh1, csc1, cg1, csh2, csc2, cg2 = jnp.split(jax.nn.silu(c_ctx) @ p['w_mod'] + p['b_mod'], 6, axis=-1)
        h = layer_norm(x) * (1.0 + sc1) + sh1
        hc = layer_norm(xc) * (1.0 + csc1) + csh1
        if i % 2 == 0:
            y, yc = mixer_fourier_mla(h, hc, rows, cols, *p['mix'], need_ctx)
        else:
            lambda_init = 0.8 - 0.6 * math.exp(-0.3 * i)
            y, yc = mixer_diff(h, hc, rows, cols, *p['mix'], lambda_init, need_ctx)
        x = layer_norm(DN_ALPHA * x + g1 * y, p['ln1_g'], p['ln1_b'])
        h = layer_norm(x) * (1.0 + sc2) + sh2
        x = layer_norm(DN_ALPHA * x + g2 * swiglu(h, p['w_gate'], p['w_up'], p['w_down']), p['ln2_g'], p['ln2_b'])
        if need_ctx:
            xc = layer_norm(DN_ALPHA * xc + cg1 * yc, p['ln1_g'], p['ln1_b'])
            hc = layer_norm(xc) * (1.0 + csc2) + csh2
            xc = layer_norm(DN_ALPHA * xc + cg2 * swiglu(hc, p['w_gate'], p['w_up'], p['w_down']), p['ln2_g'], p['ln2_b'])
    return x
```

```python
import functools
import math

import numpy as np
import jax
import jax.numpy as jnp
from jax import lax
from jax.experimental import pallas as pl
from jax.experimental.pallas import tpu as pltpu

F32 = jnp.float32
BF16 = jnp.bfloat16

D = 1024
DEPTH = 2
GRID_W = 64
ROPE_BASE = 10000.0
LN_EPS = 1e-6
RMS_EPS = 1e-6
DN_ALPHA = (2 * DEPTH) ** 0.25
LANES = 128
FOURIER_GROUPS = 4
FOURIER_GROUP_DIM = 128
FOURIER_WIDTH = 512
MLA_HEADS = 8
MLA_Q_LORA = 256
MLA_KV_LORA = 256
MLA_NOPE = 64
MLA_ROPE = 32
MLA_V = 64
MLA_SCALE = (MLA_NOPE + MLA_ROPE) ** -0.5
DIFF_HEADS = 8
DIFF_HEAD_DIM = 64
DIFF_SCALE = DIFF_HEAD_DIM ** -0.5
FF_HIDDEN = 2816
FF_CHUNK = 1408
MOD_ROWS = 8
DFT_N1 = 128
VMEM_LIMIT = 56 * 2 ** 20


def _cparams(n_axes):
    return pltpu.CompilerParams(dimension_semantics=("arbitrary",) * n_axes, vmem_limit_bytes=VMEM_LIMIT)


def _layer_norm(x):
    mu = jnp.mean(x, axis=-1, keepdims=True)
    xc = x - mu
    var = jnp.mean(xc * xc, axis=-1, keepdims=True)
    return xc * lax.rsqrt(var + LN_EPS)


def _rms(x):
    return x * lax.rsqrt(jnp.mean(x * x, axis=-1, keepdims=True) + RMS_EPS)


def _silu(x):
    return x * (1.0 / (1.0 + jnp.exp(-x)))


def _rope(x, cos, sa, sb, q):
    return x * cos + pltpu.roll(x, LANES - q, 1) * sa + pltpu.roll(x, q, 1) * sb


def _dot(a, b):
    return jnp.dot(a, b, preferred_element_type=F32)


def _dot_nt(a, b):
    return lax.dot_general(a, b, (((1,), (1,)), ((), ())), preferred_element_type=F32)


def _mod_kernel(c_ref, w_ref, b_ref, o_ref):
    o_ref[...] = _dot(_silu(c_ref[...]), w_ref[...]) + b_ref[...]


def _mod_call(c_all, w, b):
    n_out = w.shape[1]
    tn = 1024
    return pl.pallas_call(
        _mod_kernel,
        grid=(n_out // tn,),
        in_specs=[pl.BlockSpec((MOD_ROWS, D), lambda j: (0, 0)),
                  pl.BlockSpec((D, tn), lambda j: (0, j)),
                  pl.BlockSpec((1, tn), lambda j: (0, j))],
        out_specs=pl.BlockSpec((MOD_ROWS, tn), lambda j: (0, j)),
        out_shape=jax.ShapeDtypeStruct((MOD_ROWS, n_out), F32),
        compiler_params=_cparams(1),
        name="adaln_mod",
    )(c_all, w, b.reshape(1, n_out)).reshape(MOD_ROWS, 6, D)


def _rope_tables(n, rope_dim, regions):
    half, quarter = rope_dim // 2, rope_dim // 4
    is_rope = np.zeros(LANES, bool)
    use_col = np.zeros(LANES, bool)
    is_x1 = np.zeros(LANES, bool)
    fi = np.zeros(LANES, np.int32)
    for off in regions:
        for r in range(rope_dim):
            j = off + r
            rr = r % half
            is_rope[j] = True
            use_col[j] = r >= half
            is_x1[j] = rr < quarter
            fi[j] = rr % quarter
    t = jnp.arange(n, dtype=jnp.int32)
    rows = (t // GRID_W).astype(F32)[:, None]
    cols = (t % GRID_W).astype(F32)[:, None]
    inv = 1.0 / (ROPE_BASE ** (jnp.arange(quarter, dtype=F32) / quarter))
    ang = jnp.where(use_col[None, :], cols, rows) * inv[fi][None, :]
    cos = jnp.where(is_rope[None, :], jnp.cos(ang), 1.0)
    sin = jnp.sin(ang)
    sa = jnp.where((is_rope & is_x1)[None, :], -sin, 0.0)
    sb = jnp.where((is_rope & ~is_x1)[None, :], sin, 0.0)
    return cos, sa, sb


def _identity_tables(n):
    return jnp.ones((n, LANES), F32), jnp.zeros((n, LANES), F32), jnp.zeros((n, LANES), F32)


def _proj0_kernel(x_ref, mod_ref, cos_ref, sa_ref, sb_ref, wa_ref, dft_ref, qn_ref, wuq_ref, kvn_ref,
                  wk_ref, wv_ref, u_ref, q_ref, k_ref, v_ref):
    x = x_ref[0]
    m = mod_ref[0]
    h = (_layer_norm(x) * (1.0 + m[1:2]) + m[0:1]).astype(BF16)
    u = _dot(h, wa_ref[...])
    for g in range(FOURIER_GROUPS):
        lo = g * FOURIER_GROUP_DIM
        z = _dot(u[:, lo:lo + FOURIER_GROUP_DIM].astype(BF16), dft_ref[...])
        u_ref[0, 0, :, lo:lo + FOURIER_GROUP_DIM] = z[:, :FOURIER_GROUP_DIM].astype(BF16)
        u_ref[0, 1, :, lo:lo + FOURIER_GROUP_DIM] = z[:, FOURIER_GROUP_DIM:].astype(BF16)
    o1 = FOURIER_WIDTH
    o2 = o1 + MLA_Q_LORA
    o3 = o2 + MLA_KV_LORA
    cq = (_rms(u[:, o1:o2]) * qn_ref[...]).astype(BF16)
    ckv = (_rms(u[:, o2:o3]) * kvn_ref[...]).astype(BF16)
    cos, sa, sb = cos_ref[...], sa_ref[...], sb_ref[...]
    kr = _rope(u[:, o3:o3 + LANES], cos, sa, sb, MLA_ROPE // 4)
    q = _dot(cq, wuq_ref[...])
    kn = _dot(ckv, wk_ref[...])
    v = _dot(ckv, wv_ref[...])
    for hd in range(MLA_HEADS):
        sl = slice(hd * LANES, (hd + 1) * LANES)
        q_ref[0, hd] = (_rope(q[:, sl], cos, sa, sb, MLA_ROPE // 4) * MLA_SCALE).astype(BF16)
        k_ref[0, hd] = (kn[:, sl] + kr).astype(BF16)
        v_ref[0, hd] = v[:, sl].astype(BF16)


def _proj0_call(x, mod, mod_row, tables, w, tm):
    b, n, _ = x.shape
    wa, dft, qn, wuq, kvn, wk, wv = w
    const = lambda shape: pl.BlockSpec(shape, lambda bi, i: (0,) * len(shape))
    tab = pl.BlockSpec((tm, LANES), lambda bi, i: (i, 0))
    head = pl.BlockSpec((1, MLA_HEADS, tm, LANES), lambda bi, i: (bi, 0, i, 0))
    return pl.pallas_call(
        _proj0_kernel,
        grid=(b, n // tm),
        in_specs=[pl.BlockSpec((1, tm, D), lambda bi, i: (bi, i, 0)),
                  pl.BlockSpec((1, 6, D), lambda bi, i: (mod_row(bi), 0, 0)),
                  tab, tab, tab,
                  const(wa.shape), const(dft.shape), const(qn.shape), const(wuq.shape), const(kvn.shape),
                  const(wk.shape), const(wv.shape)],
        out_specs=[pl.BlockSpec((1, 2, tm, FOURIER_WIDTH), lambda bi, i: (bi, 0, i, 0)), head, head, head],
        out_shape=[jax.ShapeDtypeStruct((b, 2, n, FOURIER_WIDTH), BF16)]
        + [jax.ShapeDtypeStruct((b, MLA_HEADS, n, LANES), BF16)] * 3,
        compiler_params=_cparams(2),
        name="proj0",
    )(x, mod, *tables, wa, dft, qn, wuq, kvn, wk, wv)


def _softmax_step(q, kk, vv, carry):
    m, l, acc = carry
    s = _dot_nt(q, kk)
    m_new = jnp.maximum(m, jnp.max(s, axis=-1, keepdims=True))
    alpha = jnp.exp(m - m_new)
    p = jnp.exp(s - m_new)
    l = alpha * l + jnp.sum(p, axis=-1, keepdims=True)
    acc = alpha * acc + _dot(p.astype(BF16), vv)
    return m_new, l, acc


def _softmax_init(tq):
    return (jnp.full((tq, 1), -jnp.inf, F32), jnp.zeros((tq, 1), F32), jnp.zeros((tq, LANES), F32))


def _attend(q_list, kc, vc, k_at, v_at, n_chunks):
    tq = q_list[0].shape[0]
    carries = tuple(_softmax_step(q, kc, vc, _softmax_init(tq)) for q in q_list)
    if n_chunks:
        def body(j, cs):
            kk, vv = k_at(j), v_at(j)
            return tuple(_softmax_step(q, kk, vv, c) for q, c in zip(q_list, cs))
        carries = lax.fori_loop(0, n_chunks, body, carries)
    return [acc / l for (_, l, acc) in carries]


def _mla_attn_kernel(q_ref, kc_ref, vc_ref, *rest, tk, n_chunks, heads):
    if n_chunks:
        k_ref, v_ref, o_ref = rest
    else:
        (o_ref,) = rest
    out = None
    for hh in range(heads):
        k_at = v_at = None
        if n_chunks:
            k_at = lambda j, hh=hh: k_ref[0, hh, pl.ds(pl.multiple_of(j * tk, tk), tk), :]
            v_at = lambda j, hh=hh: v_ref[0, hh, pl.ds(pl.multiple_of(j * tk, tk), tk), :]
        (o,) = _attend([q_ref[0, hh]], kc_ref[0, hh], vc_ref[0, hh], k_at, v_at, n_chunks)
        out = o if out is None else out + o
    o_ref[0] = out.astype(BF16)


def _mla_attn_call(q, kc, vc, k, v, tq, tk):
    b, h, nq, _ = q.shape
    nc = kc.shape[2]
    heads = 2
    n_chunks = 0 if k is None else k.shape[2] // tk
    qspec = pl.BlockSpec((1, heads, tq, LANES), lambda bi, hp, i: (bi, hp, i, 0))
    full = lambda n: pl.BlockSpec((1, heads, n, LANES), lambda bi, hp, i: (bi, hp, 0, 0))
    in_specs = [qspec, full(nc), full(nc)]
    args = [q, kc, vc]
    if n_chunks:
        in_specs += [full(k.shape[2])] * 2
        args += [k, v]
    return pl.pallas_call(
        functools.partial(_mla_attn_kernel, tk=tk, n_chunks=n_chunks, heads=heads),
        grid=(b, h // heads, nq // tq),
        in_specs=in_specs,
        out_specs=pl.BlockSpec((1, tq, LANES), lambda bi, hp, i: (bi, i, hp)),
        out_shape=jax.ShapeDtypeStruct((b, nq, h // heads * LANES), BF16),
        compiler_params=_cparams(3),
        name="mla_attn",
    )(*args)


def _diff_attn_kernel(lam_ref, sub_ref, q_ref, kc_ref, vc_ref, k_ref, v_ref, o_ref, *, tk, n_chunks, lambda_init):
    lp = lam_ref[...]
    lam = (jnp.exp(jnp.sum(lp[0:1] * lp[1:2], axis=-1, keepdims=True))
           - jnp.exp(jnp.sum(lp[2:3] * lp[3:4], axis=-1, keepdims=True)) + lambda_init)
    k_at = lambda j: k_ref[0, 0, pl.ds(pl.multiple_of(j * tk, tk), tk), :]
    v_at = lambda j: v_ref[0, 0, pl.ds(pl.multiple_of(j * tk, tk), tk), :]
    o0, o1 = _attend([q_ref[0, 0, 0], q_ref[0, 0, 1]], kc_ref[0, 0], vc_ref[0, 0], k_at, v_at, n_chunks)
    o = o0 - lam * o1
    o_ref[0] = (_rms(o) * sub_ref[...] * (1.0 - lambda_init)).astype(BF16)


def _diff_attn_call(lam_params, subln, q, kc, vc, k, v, tq, tk, lambda_init):
    b, h, _, n, _ = q.shape
    nc = kc.shape[2]
    full = lambda m: pl.BlockSpec((1, 1, m, LANES), lambda bi, hd, i: (bi, hd, 0, 0))
    return pl.pallas_call(
        functools.partial(_diff_attn_kernel, tk=tk, n_chunks=n // tk, lambda_init=lambda_init),
        grid=(b, h, n // tq),
        in_specs=[pl.BlockSpec((4, DIFF_HEAD_DIM), lambda bi, hd, i: (0, 0)),
                  pl.BlockSpec((1, LANES), lambda bi, hd, i: (0, 0)),
                  pl.BlockSpec((1, 1, 2, tq, LANES), lambda bi, hd, i: (bi, hd, 0, i, 0)),
                  full(nc), full(nc), full(n), full(n)],
        out_specs=pl.BlockSpec((1, tq, LANES), lambda bi, hd, i: (bi, i, hd)),
        out_shape=jax.ShapeDtypeStruct((b, n, h * LANES), BF16),
        compiler_params=_cparams(3),
        name="diff_attn",
    )(lam_params, subln, q, kc, vc, k, v)


def _dft_stage1_kernel(u_ref, f_ref, twc_ref, tws_ref, v_ref, *, tt2):
    f = f_ref[...]
    r = _dot(f[:, :DFT_N1], u_ref[0, 0]) + _dot(f[:, DFT_N1:], u_ref[0, 1])
    vr, vi = r[:DFT_N1], r[DFT_N1:]
    twc, tws = twc_ref[0], tws_ref[0]
    for jj in range(tt2):
        c, s = twc[:, jj:jj + 1], tws[:, jj:jj + 1]
        sl = slice(jj * FOURIER_WIDTH, (jj + 1) * FOURIER_WIDTH)
        v_ref[0, 0, jj] = (vr[:, sl] * c + vi[:, sl] * s).astype(BF16)
        v_ref[0, 1, jj] = (vi[:, sl] * c - vr[:, sl] * s).astype(BF16)


def _dft_real_kernel(m_ref, v_ref, o_ref, *, kdim):
    mat = m_ref[...]
    o_ref[0] = (_dot(mat[:, :kdim], v_ref[0, 0]) + _dot(mat[:, kdim:], v_ref[0, 1])).astype(BF16)


def _dft_real_call(mat, v, tc):
    b, _, kdim, c = v.shape
    mrows = mat.shape[0]
    return pl.pallas_call(
        functools.partial(_dft_real_kernel, kdim=kdim),
        grid=(b, c // tc),
        in_specs=[pl.BlockSpec(mat.shape, lambda bi, j: (0, 0)),
                  pl.BlockSpec((1, 2, kdim, tc), lambda bi, j: (bi, 0, 0, j))],
        out_specs=pl.BlockSpec((1, mrows, tc), lambda bi, j: (bi, 0, j)),
        out_shape=jax.ShapeDtypeStruct((b, mrows, c), BF16),
        compiler_params=_cparams(2),
        name="dft_real",
    )(mat, v)


def _cos_sin(n):
    idx = np.arange(n, dtype=np.int64)
    ang = 2.0 * np.pi * ((idx[:, None] * idx[None, :]) % n) / n
    return np.cos(ang), np.sin(ang)


def _fourier_positions(u):
    b, _, n, c = u.shape
    scale = 1.0 / math.sqrt(n * FOURIER_GROUP_DIM)
    if n <= 256:
        cm, sm = _cos_sin(n)
        mat = jnp.asarray(np.concatenate([cm, sm], axis=1) * scale, F32).astype(BF16)
        return _dft_real_call(mat, u, c)
    n1, n2 = DFT_N1, n // DFT_N1
    c1, s1 = _cos_sin(n1)
    f1 = jnp.asarray(np.block([[c1, s1], [-s1, c1]]), F32).astype(BF16)
    k1 = np.arange(n1, dtype=np.int64)[:, None]
    t2 = np.arange(n2, dtype=np.int64)[None, :]
    ang = 2.0 * np.pi * ((k1 * t2) % n) / n
    tt2 = 4
    tw_shape = lambda a: jnp.asarray(a.reshape(n1, n2 // tt2, tt2).transpose(1, 0, 2), F32)
    twc, tws = tw_shape(np.cos(ang)), tw_shape(np.sin(ang))
    v = pl.pallas_call(
        functools.partial(_dft_stage1_kernel, tt2=tt2),
        grid=(n2 // tt2, b),
        in_specs=[pl.BlockSpec((1, 2, n1, tt2 * c), lambda j, bi: (bi, 0, 0, j)),
                  pl.BlockSpec((2 * n1, 2 * n1), lambda j, bi: (0, 0)),
                  pl.BlockSpec((1, n1, tt2), lambda j, bi: (j, 0, 0)),
                  pl.BlockSpec((1, n1, tt2), lambda j, bi: (j, 0, 0))],
        out_specs=pl.BlockSpec((1, 2, tt2, n1, c), lambda j, bi: (bi, 0, j, 0, 0)),
        out_shape=jax.ShapeDtypeStruct((b, 2, n2, n1, c), BF16),
        compiler_params=_cparams(2),
        name="dft_stage1",
    )(u.reshape(b, 2, n1, n2 * c), f1, twc, tws)
    c2, s2 = _cos_sin(n2)
    mat = jnp.asarray(np.concatenate([c2, s2], axis=1) * scale, F32).astype(BF16)
    y = _dft_real_call(mat, v.reshape(b, 2, n2, n1 * c), 8192)
    return y.reshape(b, n, c)


def _post_kernel(*refs, n_mix):
    x_ref, mod_ref = refs[:2]
    mix = refs[2:2 + 2 * n_mix]
    g1_ref, b1_ref, wg_ref, wu_ref, wd_ref, g2_ref, b2_ref, o_ref = refs[2 + 2 * n_mix:]
    x = x_ref[0]
    m = mod_ref[0]
    y = None
    for i in range(n_mix):
        t = _dot(mix[2 * i][0], mix[2 * i + 1][...])
        y = t if y is None else y + t
    x1 = _layer_norm(DN_ALPHA * x + m[2:3] * y) * g1_ref[...] + b1_ref[...]
    h = (_layer_norm(x1) * (1.0 + m[4:5]) + m[3:4]).astype(BF16)
    f = None
    for j in range(FF_HIDDEN // FF_CHUNK):
        sl = slice(j * FF_CHUNK, (j + 1) * FF_CHUNK)
        act = _silu(_dot(h, wg_ref[:, sl])) * _dot(h, wu_ref[:, sl])
        t = _dot(act.astype(BF16), wd_ref[sl, :])
        f = t if f is None else f + t
    o_ref[0] = _layer_norm(DN_ALPHA * x1 + m[5:6] * f) * g2_ref[...] + b2_ref[...]


def _post_call(x, mod, mod_row, mixes, g1, b1, wg, wu, wd, g2, b2, tm):
    b, n, _ = x.shape
    const = lambda a: pl.BlockSpec(a.shape, lambda bi, i: (0,) * a.ndim, pipeline_mode=pl.Buffered(1))
    in_specs = [pl.BlockSpec((1, tm, D), lambda bi, i: (bi, i, 0)),
                pl.BlockSpec((1, 6, D), lambda bi, i: (mod_row(bi), 0, 0))]
    args = [x, mod]
    for a, w in mixes:
        in_specs += [pl.BlockSpec((1, tm, a.shape[2]), lambda bi, i: (bi, i, 0)), const(w)]
        args += [a, w]
    tail = [g1, b1, wg, wu, wd, g2, b2]
    in_specs += [const(a) for a in tail]
    return pl.pallas_call(
        functools.partial(_post_kernel, n_mix=len(mixes)),
        grid=(b, n // tm),
        in_specs=in_specs,
        out_specs=pl.BlockSpec((1, tm, D), lambda bi, i: (bi, i, 0)),
        out_shape=jax.ShapeDtypeStruct((b, n, D), F32),
        compiler_params=_cparams(2),
        name="post",
    )(*args, *tail)


def _proj1_kernel(x_ref, mod_ref, cos_ref, sa_ref, sb_ref, w_ref, q_ref, k_ref, v_ref):
    x = x_ref[0]
    m = mod_ref[0]
    h = (_layer_norm(x) * (1.0 + m[1:2]) + m[0:1]).astype(BF16)
    u = _dot(h, w_ref[...])
    cos, sa, sb = cos_ref[...], sa_ref[...], sb_ref[...]
    lane = lax.broadcasted_iota(jnp.int32, (x.shape[0], LANES), 1)
    first = lane < DIFF_HEAD_DIM
    width = DIFF_HEADS * LANES
    for hd in range(DIFF_HEADS):
        sl = slice(hd * LANES, (hd + 1) * LANES)
        q = _rope(u[:, sl], cos, sa, sb, DIFF_HEAD_DIM // 4) * DIFF_SCALE
        q_ref[0, hd, 0] = jnp.where(first, q, 0.0).astype(BF16)
        q_ref[0, hd, 1] = jnp.where(first, 0.0, q).astype(BF16)
        ksl = slice(width + hd * LANES, width + (hd + 1) * LANES)
        k_ref[0, hd] = _rope(u[:, ksl], cos, sa, sb, DIFF_HEAD_DIM // 4).astype(BF16)
        vsl = slice(2 * width + hd * LANES, 2 * width + (hd + 1) * LANES)
        v_ref[0, hd] = u[:, vsl].astype(BF16)


def _proj1_call(x, mod, mod_row, tables, w, tm):
    b, n, _ = x.shape
    tab = pl.BlockSpec((tm, LANES), lambda bi, i: (i, 0))
    head = pl.BlockSpec((1, DIFF_HEADS, tm, LANES), lambda bi, i: (bi, 0, i, 0))
    return pl.pallas_call(
        _proj1_kernel,
        grid=(b, n // tm),
        in_specs=[pl.BlockSpec((1, tm, D), lambda bi, i: (bi, i, 0)),
                  pl.BlockSpec((1, 6, D), lambda bi, i: (mod_row(bi), 0, 0)),
                  tab, tab, tab,
                  pl.BlockSpec(w.shape, lambda bi, i: (0, 0))],
        out_specs=[pl.BlockSpec((1, DIFF_HEADS, 2, tm, LANES), lambda bi, i: (bi, 0, 0, i, 0)), head, head],
        out_shape=[jax.ShapeDtypeStruct((b, DIFF_HEADS, 2, n, LANES), BF16)]
        + [jax.ShapeDtypeStruct((b, DIFF_HEADS, n, LANES), BF16)] * 2,
        compiler_params=_cparams(2),
        name="proj1",
    )(x, mod, *tables, w)


def _pad_heads(w, heads, width, offset=lambda hd: 0):
    kdim = w.shape[0]
    out = jnp.zeros((kdim, heads, LANES), w.dtype)
    w3 = w.reshape(kdim, heads, width)
    for hd in range(heads):
        out = out.at[:, hd, offset(hd):offset(hd) + width].set(w3[:, hd])
    return out.reshape(kdim, heads * LANES)


def _layer0_weights(w_in, q_norm, w_uq, kv_norm, w_ukv):
    o3 = FOURIER_WIDTH + MLA_Q_LORA + MLA_KV_LORA
    w_kr = jnp.zeros((D, LANES), F32).at[:, MLA_NOPE:MLA_NOPE + MLA_ROPE].set(w_in[:, o3:])
    wa = jnp.concatenate([w_in[:, :o3], w_kr], axis=1).astype(BF16)
    cc, sc = _cos_sin(FOURIER_GROUP_DIM)
    dft = jnp.asarray(np.concatenate([cc, -sc], axis=1), F32).astype(BF16)
    wuq = _pad_heads(w_uq, MLA_HEADS, MLA_NOPE + MLA_ROPE).astype(BF16)
    ukv = w_ukv.reshape(MLA_KV_LORA, MLA_HEADS, MLA_NOPE + MLA_V)
    wk = _pad_heads(ukv[:, :, :MLA_NOPE].reshape(MLA_KV_LORA, -1), MLA_HEADS, MLA_NOPE).astype(BF16)
    wv = _pad_heads(ukv[:, :, MLA_NOPE:].reshape(MLA_KV_LORA, -1), MLA_HEADS, MLA_V,
                    offset=lambda hd: (hd % 2) * MLA_V).astype(BF16)
    return wa, dft, q_norm.reshape(1, -1), wuq, kv_norm.reshape(1, -1), wk, wv


def kernel(x, c, ctx, c_ctx,
           l0_w_mod, l0_b_mod, l0_w_in, l0_q_norm, l0_w_uq, l0_kv_norm, l0_w_ukv, l0_w_out,
           l0_ln1_g, l0_ln1_b, l0_w_gate, l0_w_up, l0_w_down, l0_ln2_g, l0_ln2_b,
           l1_w_mod, l1_b_mod, l1_w_in, l1_lambda_q1, l1_lambda_k1, l1_lambda_q2, l1_lambda_k2,
           l1_subln, l1_w_out, l1_ln1_g, l1_ln1_b, l1_w_gate, l1_w_up, l1_w_down, l1_ln2_g, l1_ln2_b):
    b, n, _ = x.shape
    nc = ctx.shape[1]
    assert b + 1 <= MOD_ROWS and n % 512 == 0 and nc % LANES == 0
    tm, tq, tk = 512, 512, 1024
    ctx_row = b
    lat_row = lambda bi: bi
    cx_row = lambda bi: ctx_row
    row = lambda a: a.reshape(1, -1)

    c_all = jnp.zeros((MOD_ROWS, D), F32).at[:b].set(c).at[b].set(c_ctx)
    mod0 = _mod_call(c_all, l0_w_mod, l0_b_mod)
    mod1 = _mod_call(c_all, l1_w_mod, l1_b_mod)
    ident = _identity_tables(nc)

    w0 = _layer0_weights(l0_w_in, l0_q_norm, l0_w_uq, l0_kv_norm, l0_w_ukv)
    tab0 = _rope_tables(n, MLA_ROPE, [MLA_NOPE])
    u, q, k, v = _proj0_call(x, mod0, lat_row, tab0, w0, tm)
    uc, qc, kc, vc = _proj0_call(ctx, mod0, cx_row, ident, w0, nc)
    att = _mla_attn_call(q, kc, vc, k, v, tq, tk)
    attc = _mla_attn_call(qc, kc, vc, None, None, nc, tk)
    four = _fourier_positions(u)
    fourc = _fourier_positions(uc)
    wo = l0_w_out.astype(BF16)
    ffn0 = (row(l0_ln1_g), row(l0_ln1_b), l0_w_gate.astype(BF16), l0_w_up.astype(BF16), l0_w_down.astype(BF16),
            row(l0_ln2_g), row(l0_ln2_b))
    x = _post_call(x, mod0, lat_row, [(four, wo[:FOURIER_WIDTH]), (att, wo[FOURIER_WIDTH:])], *ffn0, tm)
    xc = _post_call(ctx, mod0, cx_row, [(fourc, wo[:FOURIER_WIDTH]), (attc, wo[FOURIER_WIDTH:])], *ffn0, nc)

    lambda_init = 0.8 - 0.6 * math.exp(-0.3 * 1)
    w1 = l1_w_in.astype(BF16)
    tab1 = _rope_tables(n, DIFF_HEAD_DIM, [0, DIFF_HEAD_DIM])
    q, k, v = _proj1_call(x, mod1, lat_row, tab1, w1, tm)
    _, kc, vc = _proj1_call(xc, mod1, cx_row, ident, w1, nc)
    lam_params = jnp.stack([l1_lambda_q1, l1_lambda_k1, l1_lambda_q2, l1_lambda_k2])
    att = _diff_attn_call(lam_params, row(l1_subln), q, kc, vc, k, v, tq, tk, lambda_init)
    ffn1 = (row(l1_ln1_g), row(l1_ln1_b), l1_w_gate.astype(BF16), l1_w_up.astype(BF16), l1_w_down.astype(BF16),
            row(l1_ln2_g), row(l1_ln2_b))
    return _post_call(x, mod1, lat_row, [(att, l1_w_out.astype(BF16))], *ffn1, tm)
```

```python
import functools
import math

import numpy as np
import jax
import jax.numpy as jnp
from jax import lax
from jax.experimental import pallas as pl
from jax.experimental.pallas import tpu as pltpu

F32 = jnp.float32
BF16 = jnp.bfloat16

D = 1024
DEPTH = 2
GRID_W = 64
ROPE_BASE = 10000.0
LN_EPS = 1e-6
RMS_EPS = 1e-6
DN_ALPHA = (2 * DEPTH) ** 0.25
LANES = 128
FOURIER_GROUPS = 4
FOURIER_GROUP_DIM = 128
FOURIER_WIDTH = 512
MLA_HEADS = 8
MLA_Q_LORA = 256
MLA_KV_LORA = 256
MLA_NOPE = 64
MLA_ROPE = 32
MLA_V = 64
MLA_SCALE = (MLA_NOPE + MLA_ROPE) ** -0.5
DIFF_HEADS = 8
DIFF_HEAD_DIM = 64
DIFF_SCALE = DIFF_HEAD_DIM ** -0.5
LOG2E = math.log2(math.e)
FF_HIDDEN = 2816
FF_CHUNK = 1408
MOD_ROWS = 8
DFT_N1 = 128
VMEM_LIMIT = 56 * 2 ** 20


def _cparams(n_axes):
    return pltpu.CompilerParams(dimension_semantics=("arbitrary",) * n_axes, vmem_limit_bytes=VMEM_LIMIT)


def _layer_norm(x):
    mu = jnp.mean(x, axis=-1, keepdims=True)
    xc = x - mu
    var = jnp.mean(xc * xc, axis=-1, keepdims=True)
    return xc * lax.rsqrt(var + LN_EPS)


def _rms(x):
    return x * lax.rsqrt(jnp.mean(x * x, axis=-1, keepdims=True) + RMS_EPS)


def _silu(x):
    return x * (1.0 / (1.0 + jnp.exp(-x)))


def _rope(x, cos, sa, sb, q):
    return x * cos + pltpu.roll(x, LANES - q, 1) * sa + pltpu.roll(x, q, 1) * sb


def _dot(a, b):
    return jnp.dot(a, b, preferred_element_type=F32)


def _dot_nt(a, b):
    return lax.dot_general(a, b, (((1,), (1,)), ((), ())), preferred_element_type=F32)


def _mod_kernel(c_ref, w_ref, b_ref, o_ref):
    o_ref[...] = _dot(_silu(c_ref[...]), w_ref[...]) + b_ref[...]


def _mod_call(c_all, w, b):
    n_out = w.shape[1]
    tn = 1024
    return pl.pallas_call(
        _mod_kernel,
        grid=(n_out // tn,),
        in_specs=[pl.BlockSpec((MOD_ROWS, D), lambda j: (0, 0)),
                  pl.BlockSpec((D, tn), lambda j: (0, j)),
                  pl.BlockSpec((1, tn), lambda j: (0, j))],
        out_specs=pl.BlockSpec((MOD_ROWS, tn), lambda j: (0, j)),
        out_shape=jax.ShapeDtypeStruct((MOD_ROWS, n_out), F32),
        compiler_params=_cparams(1),
        name="adaln_mod",
    )(c_all, w, b.reshape(1, n_out)).reshape(MOD_ROWS, 6, D)


def _rope_tables(n, rope_dim, regions):
    half, quarter = rope_dim // 2, rope_dim // 4
    is_rope = np.zeros(LANES, bool)
    use_col = np.zeros(LANES, bool)
    is_x1 = np.zeros(LANES, bool)
    fi = np.zeros(LANES, np.int32)
    for off in regions:
        for r in range(rope_dim):
            j = off + r
            rr = r % half
            is_rope[j] = True
            use_col[j] = r >= half
            is_x1[j] = rr < quarter
            fi[j] = rr % quarter
    t = jnp.arange(n, dtype=jnp.int32)
    rows = (t // GRID_W).astype(F32)[:, None]
    cols = (t % GRID_W).astype(F32)[:, None]
    inv = 1.0 / (ROPE_BASE ** (jnp.arange(quarter, dtype=F32) / quarter))
    ang = jnp.where(use_col[None, :], cols, rows) * inv[fi][None, :]
    cos = jnp.where(is_rope[None, :], jnp.cos(ang), 1.0)
    sin = jnp.sin(ang)
    sa = jnp.where((is_rope & is_x1)[None, :], -sin, 0.0)
    sb = jnp.where((is_rope & ~is_x1)[None, :], sin, 0.0)
    return cos, sa, sb


def _identity_tables(n):
    return jnp.ones((n, LANES), F32), jnp.zeros((n, LANES), F32), jnp.zeros((n, LANES), F32)


def _proj0_kernel(x_ref, mod_ref, cos_ref, sa_ref, sb_ref, wa_ref, dft_ref, qn_ref, wuq_ref, kvn_ref,
                  wk_ref, wv_ref, u_ref, q_ref, k_ref, v_ref):
    x = x_ref[0]
    m = mod_ref[0]
    h = (_layer_norm(x) * (1.0 + m[1:2]) + m[0:1]).astype(BF16)
    u = _dot(h, wa_ref[...])
    for g in range(FOURIER_GROUPS):
        lo = g * FOURIER_GROUP_DIM
        z = _dot(u[:, lo:lo + FOURIER_GROUP_DIM].astype(BF16), dft_ref[...])
        u_ref[0, 0, :, lo:lo + FOURIER_GROUP_DIM] = z[:, :FOURIER_GROUP_DIM].astype(BF16)
        u_ref[0, 1, :, lo:lo + FOURIER_GROUP_DIM] = z[:, FOURIER_GROUP_DIM:].astype(BF16)
    o1 = FOURIER_WIDTH
    o2 = o1 + MLA_Q_LORA
    o3 = o2 + MLA_KV_LORA
    cq = (_rms(u[:, o1:o2]) * qn_ref[...]).astype(BF16)
    ckv = (_rms(u[:, o2:o3]) * kvn_ref[...]).astype(BF16)
    cos, sa, sb = cos_ref[...], sa_ref[...], sb_ref[...]
    kr = _rope(u[:, o3:o3 + LANES], cos, sa, sb, MLA_ROPE // 4)
    q = _dot(cq, wuq_ref[...])
    kn = _dot(ckv, wk_ref[...])
    v = _dot(ckv, wv_ref[...])
    ones_lane = (lax.broadcasted_iota(jnp.int32, (x.shape[0], LANES), 1) == MLA_V).astype(F32)
    for hd in range(MLA_HEADS):
        sl = slice(hd * LANES, (hd + 1) * LANES)
        q_ref[0, hd] = (_rope(q[:, sl], cos, sa, sb, MLA_ROPE // 4) * (MLA_SCALE * LOG2E)).astype(BF16)
        k_ref[0, hd] = (kn[:, sl] + kr).astype(BF16)
        v_ref[0, hd] = (v[:, sl] + ones_lane).astype(BF16)


def _proj0_call(x, mod, mod_row, tables, w, tm):
    b, n, _ = x.shape
    wa, dft, qn, wuq, kvn, wk, wv = w
    const = lambda shape: pl.BlockSpec(shape, lambda bi, i: (0,) * len(shape))
    tab = pl.BlockSpec((tm, LANES), lambda bi, i: (i, 0))
    head = pl.BlockSpec((1, MLA_HEADS, tm, LANES), lambda bi, i: (bi, 0, i, 0))
    return pl.pallas_call(
        _proj0_kernel,
        grid=(b, n // tm),
        in_specs=[pl.BlockSpec((1, tm, D), lambda bi, i: (bi, i, 0)),
                  pl.BlockSpec((1, 6, D), lambda bi, i: (mod_row(bi), 0, 0)),
                  tab, tab, tab,
                  const(wa.shape), const(dft.shape), const(qn.shape), const(wuq.shape), const(kvn.shape),
                  const(wk.shape), const(wv.shape)],
        out_specs=[pl.BlockSpec((1, 2, tm, FOURIER_WIDTH), lambda bi, i: (bi, 0, i, 0)), head, head, head],
        out_shape=[jax.ShapeDtypeStruct((b, 2, n, FOURIER_WIDTH), BF16)]
        + [jax.ShapeDtypeStruct((b, MLA_HEADS, n, LANES), BF16)] * 3,
        compiler_params=_cparams(2),
        name="proj0",
    )(x, mod, *tables, wa, dft, qn, wuq, kvn, wk, wv)


def _softmax_pv(q, kc, vc, k, v):
    s_c = _dot_nt(q, kc)
    m = jnp.max(s_c, axis=-1, keepdims=True)
    if k is not None:
        s_l = _dot_nt(q, k)
        m = jnp.maximum(m, jnp.max(s_l, axis=-1, keepdims=True))
    acc = _dot(jnp.exp2(s_c - m).astype(BF16), vc)
    if k is not None:
        acc = acc + _dot(jnp.exp2(s_l - m).astype(BF16), v)
    return acc


def _merge_head_pair(o_even, o_odd):
    lane = lax.broadcasted_iota(jnp.int32, o_even.shape, 1)
    return jnp.where(lane < MLA_V, o_even, pltpu.roll(o_odd, MLA_V, 1))


def _mla_ctx_attn_kernel(q_ref, kc_ref, vc_ref, o_ref):
    outs = []
    for hh in range(2):
        acc = _softmax_pv(q_ref[0, hh], kc_ref[0, hh], vc_ref[0, hh], None, None)
        outs.append(acc / acc[:, MLA_V:MLA_V + 1])
    o_ref[0] = _merge_head_pair(*outs).astype(BF16)


def _mla_ctx_attn_call(q, kc, vc):
    b, h, nc, _ = q.shape
    pair = pl.BlockSpec((1, 2, nc, LANES), lambda bi, hp: (bi, hp, 0, 0))
    return pl.pallas_call(
        _mla_ctx_attn_kernel,
        grid=(b, h // 2),
        in_specs=[pair, pair, pair],
        out_specs=pl.BlockSpec((1, nc, LANES), lambda bi, hp: (bi, 0, hp)),
        out_shape=jax.ShapeDtypeStruct((b, nc, h // 2 * LANES), BF16),
        compiler_params=_cparams(2),
        name="mla_ctx_attn",
    )(q, kc, vc)


def _scores_into(q, kc, k, s_ref, m_ref, slot):
    nc = kc.shape[0]
    s_c = _dot_nt(q, kc)
    s_l = _dot_nt(q, k)
    s_ref[slot, :, :nc] = s_c
    s_ref[slot, :, nc:] = s_l
    m_ref[slot] = jnp.maximum(jnp.max(s_c, axis=-1, keepdims=True), jnp.max(s_l, axis=-1, keepdims=True))


def _values_from(s_ref, m_ref, slot, vc, v):
    nc = vc.shape[0]
    m = m_ref[slot]
    return (_dot(jnp.exp2(s_ref[slot, :, :nc] - m).astype(BF16), vc)
            + _dot(jnp.exp2(s_ref[slot, :, nc:] - m).astype(BF16), v))


def _init_pipeline(scratch_refs):
    @pl.when(pl.program_id(0) == 0)
    def _():
        for r in scratch_refs:
            r[...] = jnp.zeros(r.shape, r.dtype)


def _mla_attn_kernel(q_ref, kc_ref, k_ref, vcp_ref, vp_ref, vcc_ref, vcur_ref, o_ref, s_ref, m_ref, oe_ref):
    _init_pipeline((s_ref, m_ref, oe_ref))
    _scores_into(q_ref[0, 0], kc_ref[0, 0], k_ref[0, 0], s_ref, m_ref, 0)
    acc = _values_from(s_ref, m_ref, 1, vcp_ref[0, 0], vp_ref[0, 0])
    o_ref[0] = _merge_head_pair(oe_ref[...], acc / acc[:, MLA_V:MLA_V + 1]).astype(BF16)
    _scores_into(q_ref[0, 1], kc_ref[0, 1], k_ref[0, 1], s_ref, m_ref, 1)
    acc = _values_from(s_ref, m_ref, 0, vcc_ref[0, 0], vcur_ref[0, 0])
    oe_ref[...] = acc / acc[:, MLA_V:MLA_V + 1]


def _tile_maps(n_b, n_h, n_q):
    last = n_b * n_h * n_q - 1

    def unravel(t):
        return t // (n_h * n_q), (t // n_q) % n_h, t % n_q

    cur = lambda t: unravel(jnp.minimum(t, last))
    prev = lambda t: unravel(jnp.maximum(t - 1, 0))
    return last + 2, cur, prev


def _mla_attn_call(q, kc, vc, k, v, tq):
    b, h, n, _ = q.shape
    nc = kc.shape[2]
    steps, cur, prev = _tile_maps(b, h // 2, n // tq)

    def pair(rows, blk):
        def index(t):
            bi, hp, i = cur(t)
            return (bi, hp, i if blk else 0, 0)
        return pl.BlockSpec((1, 2, rows, LANES), index)

    def one(rows, which, parity):
        def index(t):
            bi, hp, _ = which(t)
            return (bi, 2 * hp + parity, 0, 0)
        return pl.BlockSpec((1, 1, rows, LANES), index)

    def out_index(t):
        bi, hp, i = prev(t)
        return (bi, i, hp)

    return pl.pallas_call(
        _mla_attn_kernel,
        grid=(steps,),
        in_specs=[pair(tq, True), pair(nc, False), pair(n, False),
                  one(nc, prev, 1), one(n, prev, 1), one(nc, cur, 0), one(n, cur, 0)],
        out_specs=pl.BlockSpec((1, tq, LANES), out_index),
        out_shape=jax.ShapeDtypeStruct((b, n, h // 2 * LANES), BF16),
        scratch_shapes=[pltpu.VMEM((2, tq, nc + n), F32), pltpu.VMEM((2, tq, 1), F32),
                        pltpu.VMEM((tq, LANES), F32)],
        compiler_params=_cparams(1),
        name="mla_attn",
    )(q, kc, k, vc, v, vc, v)


def _diff_attn_kernel(lam_ref, sub_ref, q_ref, kc_ref, k_ref, vcp_ref, vp_ref, vcc_ref, vcur_ref, o_ref,
                      s_ref, m_ref, o0_ref, *, lambda_init):
    _init_pipeline((s_ref, m_ref, o0_ref))
    lp = lam_ref[...]
    lam = (jnp.exp(jnp.sum(lp[0:1] * lp[1:2], axis=-1, keepdims=True))
           - jnp.exp(jnp.sum(lp[2:3] * lp[3:4], axis=-1, keepdims=True)) + lambda_init)
    kc, k = kc_ref[0, 0], k_ref[0, 0]
    _scores_into(q_ref[0, 0, 0], kc, k, s_ref, m_ref, 0)
    acc = _values_from(s_ref, m_ref, 1, vcp_ref[0, 0], vp_ref[0, 0])
    o = o0_ref[...] - lam * (acc[:, :LANES] / acc[:, LANES:LANES + 1])
    o_ref[0] = (_rms(o) * sub_ref[...] * (1.0 - lambda_init)).astype(BF16)
    _scores_into(q_ref[0, 0, 1], kc, k, s_ref, m_ref, 1)
    acc = _values_from(s_ref, m_ref, 0, vcc_ref[0, 0], vcur_ref[0, 0])
    o0_ref[...] = acc[:, :LANES] / acc[:, LANES:LANES + 1]


def _diff_attn_call(lam_params, subln, q, kc, vc, k, v, tq, lambda_init):
    b, h, _, n, _ = q.shape
    nc = kc.shape[2]
    steps, cur, prev = _tile_maps(b, h, n // tq)

    def head(rows, width, which):
        def index(t):
            bi, hd, _ = which(t)
            return (bi, hd, 0, 0)
        return pl.BlockSpec((1, 1, rows, width), index)

    def q_index(t):
        bi, hd, i = cur(t)
        return (bi, hd, 0, i, 0)

    def out_index(t):
        bi, hd, i = prev(t)
        return (bi, i, hd)

    return pl.pallas_call(
        functools.partial(_diff_attn_kernel, lambda_init=lambda_init),
        grid=(steps,),
        in_specs=[pl.BlockSpec((4, DIFF_HEAD_DIM), lambda t: (0, 0)),
                  pl.BlockSpec((1, LANES), lambda t: (0, 0)),
                  pl.BlockSpec((1, 1, 2, tq, LANES), q_index),
                  head(nc, LANES, cur), head(n, LANES, cur),
                  head(nc, 2 * LANES, prev), head(n, 2 * LANES, prev),
                  head(nc, 2 * LANES, cur), head(n, 2 * LANES, cur)],
        out_specs=pl.BlockSpec((1, tq, LANES), out_index),
        out_shape=jax.ShapeDtypeStruct((b, n, h * LANES), BF16),
        scratch_shapes=[pltpu.VMEM((2, tq, nc + n), F32), pltpu.VMEM((2, tq, 1), F32),
                        pltpu.VMEM((tq, LANES), F32)],
        compiler_params=_cparams(1),
        name="diff_attn",
    )(lam_params, subln, q, kc, k, vc, v, vc, v)


def _dft_stage1_kernel(u_ref, f_ref, twc_ref, tws_ref, v_ref, *, tt2):
    f = f_ref[...]
    r = _dot(f[:, :DFT_N1], u_ref[0, 0]) + _dot(f[:, DFT_N1:], u_ref[0, 1])
    vr, vi = r[:DFT_N1], r[DFT_N1:]
    twc, tws = twc_ref[0], tws_ref[0]
    for jj in range(tt2):
        c, s = twc[:, jj:jj + 1], tws[:, jj:jj + 1]
        sl = slice(jj * FOURIER_WIDTH, (jj + 1) * FOURIER_WIDTH)
        v_ref[0, 0, jj] = (vr[:, sl] * c + vi[:, sl] * s).astype(BF16)
        v_ref[0, 1, jj] = (vi[:, sl] * c - vr[:, sl] * s).astype(BF16)


def _dft_real_kernel(m_ref, v_ref, o_ref, *, kdim):
    mat = m_ref[...]
    o_ref[0] = (_dot(mat[:, :kdim], v_ref[0, 0]) + _dot(mat[:, kdim:], v_ref[0, 1])).astype(BF16)


def _dft_real_call(mat, v, tc):
    b, _, kdim, c = v.shape
    mrows = mat.shape[0]
    return pl.pallas_call(
        functools.partial(_dft_real_kernel, kdim=kdim),
        grid=(b, c // tc),
        in_specs=[pl.BlockSpec(mat.shape, lambda bi, j: (0, 0)),
                  pl.BlockSpec((1, 2, kdim, tc), lambda bi, j: (bi, 0, 0, j))],
        out_specs=pl.BlockSpec((1, mrows, tc), lambda bi, j: (bi, 0, j)),
        out_shape=jax.ShapeDtypeStruct((b, mrows, c), BF16),
        compiler_params=_cparams(2),
        name="dft_real",
    )(mat, v)


def _cos_sin(n):
    idx = np.arange(n, dtype=np.int64)
    ang = 2.0 * np.pi * ((idx[:, None] * idx[None, :]) % n) / n
    return np.cos(ang), np.sin(ang)


def _fourier_positions(u):
    b, _, n, c = u.shape
    scale = 1.0 / math.sqrt(n * FOURIER_GROUP_DIM)
    if n <= 256:
        cm, sm = _cos_sin(n)
        mat = jnp.asarray(np.concatenate([cm, sm], axis=1) * scale, F32).astype(BF16)
        return _dft_real_call(mat, u, c)
    n1, n2 = DFT_N1, n // DFT_N1
    c1, s1 = _cos_sin(n1)
    f1 = jnp.asarray(np.block([[c1, s1], [-s1, c1]]), F32).astype(BF16)
    k1 = np.arange(n1, dtype=np.int64)[:, None]
    t2 = np.arange(n2, dtype=np.int64)[None, :]
    ang = 2.0 * np.pi * ((k1 * t2) % n) / n
    tt2 = 4
    tw_shape = lambda a: jnp.asarray(a.reshape(n1, n2 // tt2, tt2).transpose(1, 0, 2), F32)
    twc, tws = tw_shape(np.cos(ang)), tw_shape(np.sin(ang))
    v = pl.pallas_call(
        functools.partial(_dft_stage1_kernel, tt2=tt2),
        grid=(n2 // tt2, b),
        in_specs=[pl.BlockSpec((1, 2, n1, tt2 * c), lambda j, bi: (bi, 0, 0, j)),
                  pl.BlockSpec((2 * n1, 2 * n1), lambda j, bi: (0, 0)),
                  pl.BlockSpec((1, n1, tt2), lambda j, bi: (j, 0, 0)),
                  pl.BlockSpec((1, n1, tt2), lambda j, bi: (j, 0, 0))],
        out_specs=pl.BlockSpec((1, 2, tt2, n1, c), lambda j, bi: (bi, 0, j, 0, 0)),
        out_shape=jax.ShapeDtypeStruct((b, 2, n2, n1, c), BF16),
        compiler_params=_cparams(2),
        name="dft_stage1",
    )(u.reshape(b, 2, n1, n2 * c), f1, twc, tws)
    c2, s2 = _cos_sin(n2)
    mat = jnp.asarray(np.concatenate([c2, s2], axis=1) * scale, F32).astype(BF16)
    y = _dft_real_call(mat, v.reshape(b, 2, n2, n1 * c), 8192)
    return y.reshape(b, n, c)


def _post_kernel(*refs, n_mix):
    x_ref, mod_ref = refs[:2]
    mix = refs[2:2 + 2 * n_mix]
    g1_ref, b1_ref, wg_ref, wu_ref, wd_ref, g2_ref, b2_ref, o_ref = refs[2 + 2 * n_mix:]
    x = x_ref[0]
    m = mod_ref[0]
    y = None
    for i in range(n_mix):
        t = _dot(mix[2 * i][0], mix[2 * i + 1][...])
        y = t if y is None else y + t
    x1 = _layer_norm(DN_ALPHA * x + m[2:3] * y) * g1_ref[...] + b1_ref[...]
    h = (_layer_norm(x1) * (1.0 + m[4:5]) + m[3:4]).astype(BF16)
    f = None
    for j in range(FF_HIDDEN // FF_CHUNK):
        sl = slice(j * FF_CHUNK, (j + 1) * FF_CHUNK)
        act = _silu(_dot(h, wg_ref[:, sl])) * _dot(h, wu_ref[:, sl])
        t = _dot(act.astype(BF16), wd_ref[sl, :])
        f = t if f is None else f + t
    o_ref[0] = _layer_norm(DN_ALPHA * x1 + m[5:6] * f) * g2_ref[...] + b2_ref[...]


def _post_call(x, mod, mod_row, mixes, g1, b1, wg, wu, wd, g2, b2, tm):
    b, n, _ = x.shape
    const = lambda a: pl.BlockSpec(a.shape, lambda bi, i: (0,) * a.ndim, pipeline_mode=pl.Buffered(1))
    in_specs = [pl.BlockSpec((1, tm, D), lambda bi, i: (bi, i, 0)),
                pl.BlockSpec((1, 6, D), lambda bi, i: (mod_row(bi), 0, 0))]
    args = [x, mod]
    for a, w in mixes:
        in_specs += [pl.BlockSpec((1, tm, a.shape[2]), lambda bi, i: (bi, i, 0)), const(w)]
        args += [a, w]
    tail = [g1, b1, wg, wu, wd, g2, b2]
    in_specs += [const(a) for a in tail]
    return pl.pallas_call(
        functools.partial(_post_kernel, n_mix=len(mixes)),
        grid=(b, n // tm),
        in_specs=in_specs,
        out_specs=pl.BlockSpec((1, tm, D), lambda bi, i: (bi, i, 0)),
        out_shape=jax.ShapeDtypeStruct((b, n, D), F32),
        compiler_params=_cparams(2),
        name="post",
    )(*args, *tail)


def _proj1_kernel(x_ref, mod_ref, cos_ref, sa_ref, sb_ref, w_ref, q_ref, k_ref, v_ref):
    x = x_ref[0]
    m = mod_ref[0]
    h = (_layer_norm(x) * (1.0 + m[1:2]) + m[0:1]).astype(BF16)
    u = _dot(h, w_ref[...])
    cos, sa, sb = cos_ref[...], sa_ref[...], sb_ref[...]
    lane = lax.broadcasted_iota(jnp.int32, (x.shape[0], LANES), 1)
    first = lane < DIFF_HEAD_DIM
    ones_lane = (lane == 0).astype(BF16)
    width = DIFF_HEADS * LANES
    for hd in range(DIFF_HEADS):
        sl = slice(hd * LANES, (hd + 1) * LANES)
        q = _rope(u[:, sl], cos, sa, sb, DIFF_HEAD_DIM // 4) * (DIFF_SCALE * LOG2E)
        q_ref[0, hd, 0] = jnp.where(first, q, 0.0).astype(BF16)
        q_ref[0, hd, 1] = jnp.where(first, 0.0, q).astype(BF16)
        ksl = slice(width + hd * LANES, width + (hd + 1) * LANES)
        k_ref[0, hd] = _rope(u[:, ksl], cos, sa, sb, DIFF_HEAD_DIM // 4).astype(BF16)
        vsl = slice(2 * width + hd * LANES, 2 * width + (hd + 1) * LANES)
        v_ref[0, hd, :, :LANES] = u[:, vsl].astype(BF16)
        v_ref[0, hd, :, LANES:] = ones_lane


def _proj1_call(x, mod, mod_row, tables, w, tm):
    b, n, _ = x.shape
    tab = pl.BlockSpec((tm, LANES), lambda bi, i: (i, 0))
    head = pl.BlockSpec((1, DIFF_HEADS, tm, LANES), lambda bi, i: (bi, 0, i, 0))
    return pl.pallas_call(
        _proj1_kernel,
        grid=(b, n // tm),
        in_specs=[pl.BlockSpec((1, tm, D), lambda bi, i: (bi, i, 0)),
                  pl.BlockSpec((1, 6, D), lambda bi, i: (mod_row(bi), 0, 0)),
                  tab, tab, tab,
                  pl.BlockSpec(w.shape, lambda bi, i: (0, 0))],
        out_specs=[pl.BlockSpec((1, DIFF_HEADS, 2, tm, LANES), lambda bi, i: (bi, 0, 0, i, 0)), head,
                   pl.BlockSpec((1, DIFF_HEADS, tm, 2 * LANES), lambda bi, i: (bi, 0, i, 0))],
        out_shape=[jax.ShapeDtypeStruct((b, DIFF_HEADS, 2, n, LANES), BF16),
                   jax.ShapeDtypeStruct((b, DIFF_HEADS, n, LANES), BF16),
                   jax.ShapeDtypeStruct((b, DIFF_HEADS, n, 2 * LANES), BF16)],
        compiler_params=_cparams(2),
        name="proj1",
    )(x, mod, *tables, w)


def _pad_heads(w, heads, width):
    kdim = w.shape[0]
    w3 = jnp.pad(w.reshape(kdim, heads, width), ((0, 0), (0, 0), (0, LANES - width)))
    return w3.reshape(kdim, heads * LANES)


def _layer0_weights(w_in, q_norm, w_uq, kv_norm, w_ukv):
    o3 = FOURIER_WIDTH + MLA_Q_LORA + MLA_KV_LORA
    w_kr = jnp.zeros((D, LANES), F32).at[:, MLA_NOPE:MLA_NOPE + MLA_ROPE].set(w_in[:, o3:])
    wa = jnp.concatenate([w_in[:, :o3], w_kr], axis=1).astype(BF16)
    cc, sc = _cos_sin(FOURIER_GROUP_DIM)
    dft = jnp.asarray(np.concatenate([cc, -sc], axis=1), F32).astype(BF16)
    wuq = _pad_heads(w_uq, MLA_HEADS, MLA_NOPE + MLA_ROPE).astype(BF16)
    ukv = w_ukv.reshape(MLA_KV_LORA, MLA_HEADS, MLA_NOPE + MLA_V)
    wk = _pad_heads(ukv[:, :, :MLA_NOPE].reshape(MLA_KV_LORA, -1), MLA_HEADS, MLA_NOPE).astype(BF16)
    wv = _pad_heads(ukv[:, :, MLA_NOPE:].reshape(MLA_KV_LORA, -1), MLA_HEADS, MLA_V).astype(BF16)
    return wa, dft, q_norm.reshape(1, -1), wuq, kv_norm.reshape(1, -1), wk, wv


def kernel(x, c, ctx, c_ctx,
           l0_w_mod, l0_b_mod, l0_w_in, l0_q_norm, l0_w_uq, l0_kv_norm, l0_w_ukv, l0_w_out,
           l0_ln1_g, l0_ln1_b, l0_w_gate, l0_w_up, l0_w_down, l0_ln2_g, l0_ln2_b,
           l1_w_mod, l1_b_mod, l1_w_in, l1_lambda_q1, l1_lambda_k1, l1_lambda_q2, l1_lambda_k2,
           l1_subln, l1_w_out, l1_ln1_g, l1_ln1_b, l1_w_gate, l1_w_up, l1_w_down, l1_ln2_g, l1_ln2_b):
    b, n, _ = x.shape
    nc = ctx.shape[1]
    assert b + 1 <= MOD_ROWS and n % 512 == 0 and nc % LANES == 0
    tm, tq = 512, 256
    ctx_row = b
    lat_row = lambda bi: bi
    cx_row = lambda bi: ctx_row
    row = lambda a: a.reshape(1, -1)

    c_all = jnp.zeros((MOD_ROWS, D), F32).at[:b].set(c).at[b].set(c_ctx)
    mod0 = _mod_call(c_all, l0_w_mod, l0_b_mod)
    mod1 = _mod_call(c_all, l1_w_mod, l1_b_mod)
    ident = _identity_tables(nc)

    w0 = _layer0_weights(l0_w_in, l0_q_norm, l0_w_uq, l0_kv_norm, l0_w_ukv)
    tab0 = _rope_tables(n, MLA_ROPE, [MLA_NOPE])
    u, q, k, v = _proj0_call(x, mod0, lat_row, tab0, w0, tm)
    uc, qc, kc, vc = _proj0_call(ctx, mod0, cx_row, ident, w0, nc)
    att = _mla_attn_call(q, kc, vc, k, v, tq)
    attc = _mla_ctx_attn_call(qc, kc, vc)
    four = _fourier_positions(u)
    fourc = _fourier_positions(uc)
    wo = l0_w_out.astype(BF16)
    ffn0 = (row(l0_ln1_g), row(l0_ln1_b), l0_w_gate.astype(BF16), l0_w_up.astype(BF16), l0_w_down.astype(BF16),
            row(l0_ln2_g), row(l0_ln2_b))
    x = _post_call(x, mod0, lat_row, [(four, wo[:FOURIER_WIDTH]), (att, wo[FOURIER_WIDTH:])], *ffn0, tm)
    xc = _post_call(ctx, mod0, cx_row, [(fourc, wo[:FOURIER_WIDTH]), (attc, wo[FOURIER_WIDTH:])], *ffn0, nc)

    lambda_init = 0.8 - 0.6 * math.exp(-0.3 * 1)
    w1 = l1_w_in.astype(BF16)
    tab1 = _rope_tables(n, DIFF_HEAD_DIM, [0, DIFF_HEAD_DIM])
    q, k, v = _proj1_call(x, mod1, lat_row, tab1, w1, tm)
    _, kc, vc = _proj1_call(xc, mod1, cx_row, ident, w1, nc)
    lam_params = jnp.stack([l1_lambda_q1, l1_lambda_k1, l1_lambda_q2, l1_lambda_k2])
    att = _diff_attn_call(lam_params, row(l1_subln), q, kc, vc, k, v, tq, lambda_init)
    ffn1 = (row(l1_ln1_g), row(l1_ln1_b), l1_w_gate.astype(BF16), l1_w_up.astype(BF16), l1_w_down.astype(BF16),
            row(l1_ln2_g), row(l1_ln2_b))
    return _post_call(x, mod1, lat_row, [(att, l1_w_out.astype(BF16))], *ffn1, tm)
```

```python
import functools
import math

import numpy as np
import jax
import jax.numpy as jnp
from jax import lax
from jax.experimental import pallas as pl
from jax.experimental.pallas import tpu as pltpu

F32 = jnp.float32
BF16 = jnp.bfloat16

D = 1024
DEPTH = 2
GRID_W = 64
ROPE_BASE = 10000.0
LN_EPS = 1e-6
RMS_EPS = 1e-6
DN_ALPHA = (2 * DEPTH) ** 0.25
LANES = 128
FOURIER_GROUPS = 4
FOURIER_GROUP_DIM = 128
FOURIER_WIDTH = 512
MLA_HEADS = 8
MLA_Q_LORA = 256
MLA_KV_LORA = 256
MLA_NOPE = 64
MLA_ROPE = 32
MLA_V = 64
MLA_SCALE = (MLA_NOPE + MLA_ROPE) ** -0.5
DIFF_HEADS = 8
DIFF_HEAD_DIM = 64
DIFF_SCALE = DIFF_HEAD_DIM ** -0.5
LOG2E = math.log2(math.e)
MXU_TILE = 256
FF_HIDDEN = 2816
FF_CHUNKS = (1536, 1280)
ROW_SUBTILES = 2
MOD_ROWS = 8
DFT_N1 = 128
VMEM_LIMIT = 56 * 2 ** 20


def _cparams(n_axes):
    return pltpu.CompilerParams(dimension_semantics=("arbitrary",) * n_axes, vmem_limit_bytes=VMEM_LIMIT)


def _layer_norm(x):
    mu = jnp.mean(x, axis=-1, keepdims=True)
    xc = x - mu
    var = jnp.mean(xc * xc, axis=-1, keepdims=True)
    return xc * lax.rsqrt(var + LN_EPS)


def _rms(x):
    return x * lax.rsqrt(jnp.mean(x * x, axis=-1, keepdims=True) + RMS_EPS)


def _silu(x):
    return x * (1.0 / (1.0 + jnp.exp(-x)))


def _rope(x, cos, sa, sb, q):
    return x * cos + pltpu.roll(x, LANES - q, 1) * sa + pltpu.roll(x, q, 1) * sb


def _row_subtiles(tm):
    n_sub = ROW_SUBTILES if tm % (ROW_SUBTILES * MXU_TILE) == 0 else 1
    return [slice(r * (tm // n_sub), (r + 1) * (tm // n_sub)) for r in range(n_sub)]


def _dot(a, b):
    return jnp.dot(a, b, preferred_element_type=F32)


def _dot_nt(a, b):
    return lax.dot_general(a, b, (((1,), (1,)), ((), ())), preferred_element_type=F32)


def _mod_kernel(c_ref, w_ref, b_ref, o_ref):
    o_ref[...] = _dot(_silu(c_ref[...]), w_ref[...]) + b_ref[...]


def _mod_call(c_all, w, b):
    n_out = w.shape[1]
    tn = 1024
    return pl.pallas_call(
        _mod_kernel,
        grid=(n_out // tn,),
        in_specs=[pl.BlockSpec((MOD_ROWS, D), lambda j: (0, 0)),
                  pl.BlockSpec((D, tn), lambda j: (0, j)),
                  pl.BlockSpec((1, tn), lambda j: (0, j))],
        out_specs=pl.BlockSpec((MOD_ROWS, tn), lambda j: (0, j)),
        out_shape=jax.ShapeDtypeStruct((MOD_ROWS, n_out), F32),
        compiler_params=_cparams(1),
        name="adaln_mod",
    )(c_all, w, b.reshape(1, n_out)).reshape(MOD_ROWS, 6, D)


def _rope_tables(n, rope_dim, regions):
    half, quarter = rope_dim // 2, rope_dim // 4
    is_rope = np.zeros(LANES, bool)
    use_col = np.zeros(LANES, bool)
    is_x1 = np.zeros(LANES, bool)
    fi = np.zeros(LANES, np.int32)
    for off in regions:
        for r in range(rope_dim):
            j = off + r
            rr = r % half
            is_rope[j] = True
            use_col[j] = r >= half
            is_x1[j] = rr < quarter
            fi[j] = rr % quarter
    t = jnp.arange(n, dtype=jnp.int32)
    rows = (t // GRID_W).astype(F32)[:, None]
    cols = (t % GRID_W).astype(F32)[:, None]
    inv = 1.0 / (ROPE_BASE ** (jnp.arange(quarter, dtype=F32) / quarter))
    ang = jnp.where(use_col[None, :], cols, rows) * inv[fi][None, :]
    cos = jnp.where(is_rope[None, :], jnp.cos(ang), 1.0)
    sin = jnp.sin(ang)
    sa = jnp.where((is_rope & is_x1)[None, :], -sin, 0.0)
    sb = jnp.where((is_rope & ~is_x1)[None, :], sin, 0.0)
    return cos, sa, sb


def _identity_tables(n):
    return jnp.ones((n, LANES), F32), jnp.zeros((n, LANES), F32), jnp.zeros((n, LANES), F32)


def _proj0_kernel(x_ref, mod_ref, cos_ref, sa_ref, sb_ref, wa_ref, dft_ref, qn_ref, wuq_ref, kvn_ref,
                  wk_ref, wv_ref, u_ref, q_ref, k_ref, v_ref):
    m = mod_ref[0]
    sub = _row_subtiles(x_ref.shape[1])
    hs = [(_layer_norm(x_ref[0, rows]) * (1.0 + m[1:2]) + m[0:1]).astype(BF16) for rows in sub]
    us = [_dot(h, wa_ref[...]) for h in hs]
    o1 = FOURIER_WIDTH
    o2 = o1 + MLA_Q_LORA
    o3 = o2 + MLA_KV_LORA
    for rows, u in zip(sub, us):
        for g in range(FOURIER_GROUPS):
            lo = g * FOURIER_GROUP_DIM
            z = _dot(u[:, lo:lo + FOURIER_GROUP_DIM].astype(BF16), dft_ref[...])
            u_ref[0, 0, rows, lo:lo + FOURIER_GROUP_DIM] = z[:, :FOURIER_GROUP_DIM].astype(BF16)
            u_ref[0, 1, rows, lo:lo + FOURIER_GROUP_DIM] = z[:, FOURIER_GROUP_DIM:].astype(BF16)
    cqs = [(_rms(u[:, o1:o2]) * qn_ref[...]).astype(BF16) for u in us]
    ckvs = [(_rms(u[:, o2:o3]) * kvn_ref[...]).astype(BF16) for u in us]
    qs = [_dot(cq, wuq_ref[...]) for cq in cqs]
    kns = [_dot(ckv, wk_ref[...]) for ckv in ckvs]
    vs = [_dot(ckv, wv_ref[...]) for ckv in ckvs]
    for rows, u, q, kn, v in zip(sub, us, qs, kns, vs):
        cos, sa, sb = cos_ref[rows], sa_ref[rows], sb_ref[rows]
        kr = _rope(u[:, o3:o3 + LANES], cos, sa, sb, MLA_ROPE // 4)
        ones_lane = (lax.broadcasted_iota(jnp.int32, (u.shape[0], LANES), 1) == MLA_V).astype(F32)
        for hd in range(MLA_HEADS):
            sl = slice(hd * LANES, (hd + 1) * LANES)
            q_ref[0, hd, rows] = (_rope(q[:, sl], cos, sa, sb, MLA_ROPE // 4) * (MLA_SCALE * LOG2E)).astype(BF16)
            k_ref[0, hd, rows] = (kn[:, sl] + kr).astype(BF16)
            v_ref[0, hd, rows] = (v[:, sl] + ones_lane).astype(BF16)


def _proj0_call(x, mod, mod_row, tables, w, tm):
    b, n, _ = x.shape
    wa, dft, qn, wuq, kvn, wk, wv = w
    const = lambda shape: pl.BlockSpec(shape, lambda bi, i: (0,) * len(shape))
    tab = pl.BlockSpec((tm, LANES), lambda bi, i: (i, 0))
    head = pl.BlockSpec((1, MLA_HEADS, tm, LANES), lambda bi, i: (bi, 0, i, 0))
    return pl.pallas_call(
        _proj0_kernel,
        grid=(b, n // tm),
        in_specs=[pl.BlockSpec((1, tm, D), lambda bi, i: (bi, i, 0)),
                  pl.BlockSpec((1, 6, D), lambda bi, i: (mod_row(bi), 0, 0)),
                  tab, tab, tab,
                  const(wa.shape), const(dft.shape), const(qn.shape), const(wuq.shape), const(kvn.shape),
                  const(wk.shape), const(wv.shape)],
        out_specs=[pl.BlockSpec((1, 2, tm, FOURIER_WIDTH), lambda bi, i: (bi, 0, i, 0)), head, head, head],
        out_shape=[jax.ShapeDtypeStruct((b, 2, n, FOURIER_WIDTH), BF16)]
        + [jax.ShapeDtypeStruct((b, MLA_HEADS, n, LANES), BF16)] * 3,
        compiler_params=_cparams(2),
        name="proj0",
    )(x, mod, *tables, wa, dft, qn, wuq, kvn, wk, wv)


def _softmax_pv(q, kc, vc, k, v):
    s_c = _dot_nt(q, kc)
    m = jnp.max(s_c, axis=-1, keepdims=True)
    if k is not None:
        s_l = _dot_nt(q, k)
        m = jnp.maximum(m, jnp.max(s_l, axis=-1, keepdims=True))
    acc = _dot(jnp.exp2(s_c - m).astype(BF16), vc)
    if k is not None:
        acc = acc + _dot(jnp.exp2(s_l - m).astype(BF16), v)
    return acc


def _merge_head_pair(o_even, o_odd):
    lane = lax.broadcasted_iota(jnp.int32, o_even.shape, 1)
    return jnp.where(lane < MLA_V, o_even, pltpu.roll(o_odd, MLA_V, 1))


def _mla_ctx_attn_kernel(q_ref, kc_ref, vc_ref, o_ref):
    outs = []
    for hh in range(2):
        acc = _softmax_pv(q_ref[0, hh], kc_ref[0, hh], vc_ref[0, hh], None, None)
        outs.append(acc / acc[:, MLA_V:MLA_V + 1])
    o_ref[0] = _merge_head_pair(*outs).astype(BF16)


def _mla_ctx_attn_call(q, kc, vc):
    b, h, nc, _ = q.shape
    pair = pl.BlockSpec((1, 2, nc, LANES), lambda bi, hp: (bi, hp, 0, 0))
    return pl.pallas_call(
        _mla_ctx_attn_kernel,
        grid=(b, h // 2),
        in_specs=[pair, pair, pair],
        out_specs=pl.BlockSpec((1, nc, LANES), lambda bi, hp: (bi, 0, hp)),
        out_shape=jax.ShapeDtypeStruct((b, nc, h // 2 * LANES), BF16),
        compiler_params=_cparams(2),
        name="mla_ctx_attn",
    )(q, kc, vc)


def _scores_into(q, kc, k, s_ref, m_ref, slot):
    nc = kc.shape[0]
    s_c = _dot_nt(q, kc)
    s_l = _dot_nt(q, k)
    s_ref[slot, :, :nc] = s_c
    s_ref[slot, :, nc:] = s_l
    m_ref[slot] = jnp.maximum(jnp.max(s_c, axis=-1, keepdims=True), jnp.max(s_l, axis=-1, keepdims=True))


def _values_from(s_ref, m_ref, slot, vc, v):
    nc = vc.shape[0]
    m = m_ref[slot]
    return (_dot(jnp.exp2(s_ref[slot, :, :nc] - m).astype(BF16), vc)
            + _dot(jnp.exp2(s_ref[slot, :, nc:] - m).astype(BF16), v))


def _init_pipeline(scratch_refs):
    @pl.when(pl.program_id(0) == 0)
    def _():
        for r in scratch_refs:
            r[...] = jnp.zeros(r.shape, r.dtype)


def _mla_attn_kernel(q_ref, kc_ref, k_ref, vcp_ref, vp_ref, vcc_ref, vcur_ref, o_ref, s_ref, m_ref, oe_ref):
    _init_pipeline((s_ref, m_ref, oe_ref))
    _scores_into(q_ref[0, 0], kc_ref[0, 0], k_ref[0, 0], s_ref, m_ref, 0)
    acc = _values_from(s_ref, m_ref, 1, vcp_ref[0, 0], vp_ref[0, 0])
    o_ref[0] = _merge_head_pair(oe_ref[...], acc / acc[:, MLA_V:MLA_V + 1]).astype(BF16)
    _scores_into(q_ref[0, 1], kc_ref[0, 1], k_ref[0, 1], s_ref, m_ref, 1)
    acc = _values_from(s_ref, m_ref, 0, vcc_ref[0, 0], vcur_ref[0, 0])
    oe_ref[...] = acc / acc[:, MLA_V:MLA_V + 1]


def _tile_maps(n_b, n_h, n_q):
    last = n_b * n_h * n_q - 1

    def unravel(t):
        return t // (n_h * n_q), (t // n_q) % n_h, t % n_q

    cur = lambda t: unravel(jnp.minimum(t, last))
    prev = lambda t: unravel(jnp.maximum(t - 1, 0))
    return last + 2, cur, prev


def _mla_attn_call(q, kc, vc, k, v, tq):
    b, h, n, _ = q.shape
    nc = kc.shape[2]
    steps, cur, prev = _tile_maps(b, h // 2, n // tq)

    def pair(rows, blk):
        def index(t):
            bi, hp, i = cur(t)
            return (bi, hp, i if blk else 0, 0)
        return pl.BlockSpec((1, 2, rows, LANES), index)

    def one(rows, which, parity):
        def index(t):
            bi, hp, _ = which(t)
            return (bi, 2 * hp + parity, 0, 0)
        return pl.BlockSpec((1, 1, rows, LANES), index)

    def out_index(t):
        bi, hp, i = prev(t)
        return (bi, i, hp)

    return pl.pallas_call(
        _mla_attn_kernel,
        grid=(steps,),
        in_specs=[pair(tq, True), pair(nc, False), pair(n, False),
                  one(nc, prev, 1), one(n, prev, 1), one(nc, cur, 0), one(n, cur, 0)],
        out_specs=pl.BlockSpec((1, tq, LANES), out_index),
        out_shape=jax.ShapeDtypeStruct((b, n, h // 2 * LANES), BF16),
        scratch_shapes=[pltpu.VMEM((2, tq, nc + n), F32), pltpu.VMEM((2, tq, 1), F32),
                        pltpu.VMEM((tq, LANES), F32)],
        compiler_params=_cparams(1),
        name="mla_attn",
    )(q, kc, k, vc, v, vc, v)


def _diff_attn_kernel(lam_ref, sub_ref, q_ref, kc_ref, k_ref, vcp_ref, vp_ref, vcc_ref, vcur_ref, o_ref,
                      s_ref, m_ref, o0_ref, *, lambda_init):
    _init_pipeline((s_ref, m_ref, o0_ref))
    lp = lam_ref[...]
    lam = (jnp.exp(jnp.sum(lp[0:1] * lp[1:2], axis=-1, keepdims=True))
           - jnp.exp(jnp.sum(lp[2:3] * lp[3:4], axis=-1, keepdims=True)) + lambda_init)
    kc, k = kc_ref[0, 0], k_ref[0, 0]
    _scores_into(q_ref[0, 0, 0], kc, k, s_ref, m_ref, 0)
    acc = _values_from(s_ref, m_ref, 1, vcp_ref[0, 0], vp_ref[0, 0])
    o = o0_ref[...] - lam * (acc[:, :LANES] / acc[:, LANES:LANES + 1])
    o_ref[0] = (_rms(o) * sub_ref[...] * (1.0 - lambda_init)).astype(BF16)
    _scores_into(q_ref[0, 0, 1], kc, k, s_ref, m_ref, 1)
    acc = _values_from(s_ref, m_ref, 0, vcc_ref[0, 0], vcur_ref[0, 0])
    o0_ref[...] = acc[:, :LANES] / acc[:, LANES:LANES + 1]


def _diff_attn_call(lam_params, subln, q, kc, vc, k, v, tq, lambda_init):
    b, h, _, n, _ = q.shape
    nc = kc.shape[2]
    steps, cur, prev = _tile_maps(b, h, n // tq)

    def head(rows, width, which):
        def index(t):
            bi, hd, _ = which(t)
            return (bi, hd, 0, 0)
        return pl.BlockSpec((1, 1, rows, width), index)

    def q_index(t):
        bi, hd, i = cur(t)
        return (bi, hd, 0, i, 0)

    def out_index(t):
        bi, hd, i = prev(t)
        return (bi, i, hd)

    return pl.pallas_call(
        functools.partial(_diff_attn_kernel, lambda_init=lambda_init),
        grid=(steps,),
        in_specs=[pl.BlockSpec((4, DIFF_HEAD_DIM), lambda t: (0, 0)),
                  pl.BlockSpec((1, LANES), lambda t: (0, 0)),
                  pl.BlockSpec((1, 1, 2, tq, LANES), q_index),
                  head(nc, LANES, cur), head(n, LANES, cur),
                  head(nc, 2 * LANES, prev), head(n, 2 * LANES, prev),
                  head(nc, 2 * LANES, cur), head(n, 2 * LANES, cur)],
        out_specs=pl.BlockSpec((1, tq, LANES), out_index),
        out_shape=jax.ShapeDtypeStruct((b, n, h * LANES), BF16),
        scratch_shapes=[pltpu.VMEM((2, tq, nc + n), F32), pltpu.VMEM((2, tq, 1), F32),
                        pltpu.VMEM((tq, LANES), F32)],
        compiler_params=_cparams(1),
        name="diff_attn",
    )(lam_params, subln, q, kc, k, vc, v, vc, v)


def _dft_stage1_kernel(u_ref, f_ref, twc_ref, tws_ref, v_ref, *, tt2):
    f = f_ref[...]
    r = _dot(f[:, :DFT_N1], u_ref[0, 0]) + _dot(f[:, DFT_N1:], u_ref[0, 1])
    vr, vi = r[:DFT_N1], r[DFT_N1:]
    twc, tws = twc_ref[0], tws_ref[0]
    for jj in range(tt2):
        c, s = twc[:, jj:jj + 1], tws[:, jj:jj + 1]
        sl = slice(jj * FOURIER_WIDTH, (jj + 1) * FOURIER_WIDTH)
        v_ref[0, 0, jj] = (vr[:, sl] * c + vi[:, sl] * s).astype(BF16)
        v_ref[0, 1, jj] = (vi[:, sl] * c - vr[:, sl] * s).astype(BF16)


def _dft_real_kernel(m_ref, v_ref, o_ref, *, kdim):
    mat = m_ref[...]
    o_ref[0] = (_dot(mat[:, :kdim], v_ref[0, 0]) + _dot(mat[:, kdim:], v_ref[0, 1])).astype(BF16)


def _dft_real_call(mat, v, tc):
    b, _, kdim, c = v.shape
    mrows = mat.shape[0]
    return pl.pallas_call(
        functools.partial(_dft_real_kernel, kdim=kdim),
        grid=(b, c // tc),
        in_specs=[pl.BlockSpec(mat.shape, lambda bi, j: (0, 0)),
                  pl.BlockSpec((1, 2, kdim, tc), lambda bi, j: (bi, 0, 0, j))],
        out_specs=pl.BlockSpec((1, mrows, tc), lambda bi, j: (bi, 0, j)),
        out_shape=jax.ShapeDtypeStruct((b, mrows, c), BF16),
        compiler_params=_cparams(2),
        name="dft_real",
    )(mat, v)


def _cos_sin(n):
    idx = np.arange(n, dtype=np.int64)
    ang = 2.0 * np.pi * ((idx[:, None] * idx[None, :]) % n) / n
    return np.cos(ang), np.sin(ang)


def _fourier_positions(u):
    b, _, n, c = u.shape
    scale = 1.0 / math.sqrt(n * FOURIER_GROUP_DIM)
    if n <= 256:
        cm, sm = _cos_sin(n)
        mat = jnp.asarray(np.concatenate([cm, sm], axis=1) * scale, F32).astype(BF16)
        return _dft_real_call(mat, u, c)
    n1, n2 = DFT_N1, n // DFT_N1
    c1, s1 = _cos_sin(n1)
    f1 = jnp.asarray(np.block([[c1, s1], [-s1, c1]]), F32).astype(BF16)
    k1 = np.arange(n1, dtype=np.int64)[:, None]
    t2 = np.arange(n2, dtype=np.int64)[None, :]
    ang = 2.0 * np.pi * ((k1 * t2) % n) / n
    tt2 = 4
    tw_shape = lambda a: jnp.asarray(a.reshape(n1, n2 // tt2, tt2).transpose(1, 0, 2), F32)
    twc, tws = tw_shape(np.cos(ang)), tw_shape(np.sin(ang))
    v = pl.pallas_call(
        functools.partial(_dft_stage1_kernel, tt2=tt2),
        grid=(n2 // tt2, b),
        in_specs=[pl.BlockSpec((1, 2, n1, tt2 * c), lambda j, bi: (bi, 0, 0, j)),
                  pl.BlockSpec((2 * n1, 2 * n1), lambda j, bi: (0, 0)),
                  pl.BlockSpec((1, n1, tt2), lambda j, bi: (j, 0, 0)),
                  pl.BlockSpec((1, n1, tt2), lambda j, bi: (j, 0, 0))],
        out_specs=pl.BlockSpec((1, 2, tt2, n1, c), lambda j, bi: (bi, 0, j, 0, 0)),
        out_shape=jax.ShapeDtypeStruct((b, 2, n2, n1, c), BF16),
        compiler_params=_cparams(2),
        name="dft_stage1",
    )(u.reshape(b, 2, n1, n2 * c), f1, twc, tws)
    c2, s2 = _cos_sin(n2)
    mat = jnp.asarray(np.concatenate([c2, s2], axis=1) * scale, F32).astype(BF16)
    y = _dft_real_call(mat, v.reshape(b, 2, n2, n1 * c), 8192)
    return y.reshape(b, n, c)


def _post_kernel(*refs, n_mix):
    x_ref, mod_ref = refs[:2]
    mix = refs[2:2 + 2 * n_mix]
    g1_ref, b1_ref, wg_ref, wu_ref, wd_ref, g2_ref, b2_ref, o_ref = refs[2 + 2 * n_mix:]
    m = mod_ref[0]
    sub = _row_subtiles(x_ref.shape[1])
    x1s, hs, fs = [], [], []
    for rows in sub:
        y = None
        for i in range(n_mix):
            t = _dot(mix[2 * i][0, rows], mix[2 * i + 1][...])
            y = t if y is None else y + t
        x1 = _layer_norm(DN_ALPHA * x_ref[0, rows] + m[2:3] * y) * g1_ref[...] + b1_ref[...]
        x1s.append(x1)
        hs.append((_layer_norm(x1) * (1.0 + m[4:5]) + m[3:4]).astype(BF16))
    for h in hs:
        f = None
        lo = 0
        for width in FF_CHUNKS:
            sl = slice(lo, lo + width)
            lo += width
            act = _silu(_dot(h, wg_ref[:, sl])) * _dot(h, wu_ref[:, sl])
            t = _dot(act.astype(BF16), wd_ref[sl, :])
            f = t if f is None else f + t
        fs.append(f)
    for rows, x1, f in zip(sub, x1s, fs):
        o_ref[0, rows] = _layer_norm(DN_ALPHA * x1 + m[5:6] * f) * g2_ref[...] + b2_ref[...]


def _post_call(x, mod, mod_row, mixes, g1, b1, wg, wu, wd, g2, b2, tm):
    b, n, _ = x.shape
    const = lambda a: pl.BlockSpec(a.shape, lambda bi, i: (0,) * a.ndim, pipeline_mode=pl.Buffered(1))
    in_specs = [pl.BlockSpec((1, tm, D), lambda bi, i: (bi, i, 0)),
                pl.BlockSpec((1, 6, D), lambda bi, i: (mod_row(bi), 0, 0))]
    args = [x, mod]
    for a, w in mixes:
        in_specs += [pl.BlockSpec((1, tm, a.shape[2]), lambda bi, i: (bi, i, 0)), const(w)]
        args += [a, w]
    tail = [g1, b1, wg, wu, wd, g2, b2]
    in_specs += [const(a) for a in tail]
    return pl.pallas_call(
        functools.partial(_post_kernel, n_mix=len(mixes)),
        grid=(b, n // tm),
        in_specs=in_specs,
        out_specs=pl.BlockSpec((1, tm, D), lambda bi, i: (bi, i, 0)),
        out_shape=jax.ShapeDtypeStruct((b, n, D), F32),
        compiler_params=_cparams(2),
        name="post",
    )(*args, *tail)


def _proj1_kernel(x_ref, mod_ref, cos_ref, sa_ref, sb_ref, w_ref, q_ref, k_ref, v_ref):
    m = mod_ref[0]
    sub = _row_subtiles(x_ref.shape[1])
    hs = [(_layer_norm(x_ref[0, rows]) * (1.0 + m[1:2]) + m[0:1]).astype(BF16) for rows in sub]
    us = [_dot(h, w_ref[...]) for h in hs]
    width = DIFF_HEADS * LANES
    for rows, u in zip(sub, us):
        cos, sa, sb = cos_ref[rows], sa_ref[rows], sb_ref[rows]
        lane = lax.broadcasted_iota(jnp.int32, (u.shape[0], LANES), 1)
        first = lane < DIFF_HEAD_DIM
        ones_lane = (lane == 0).astype(BF16)
        for hd in range(DIFF_HEADS):
            sl = slice(hd * LANES, (hd + 1) * LANES)
            q = _rope(u[:, sl], cos, sa, sb, DIFF_HEAD_DIM // 4) * (DIFF_SCALE * LOG2E)
            q_ref[0, hd, 0, rows] = jnp.where(first, q, 0.0).astype(BF16)
            q_ref[0, hd, 1, rows] = jnp.where(first, 0.0, q).astype(BF16)
            ksl = slice(width + hd * LANES, width + (hd + 1) * LANES)
            k_ref[0, hd, rows] = _rope(u[:, ksl], cos, sa, sb, DIFF_HEAD_DIM // 4).astype(BF16)
            vsl = slice(2 * width + hd * LANES, 2 * width + (hd + 1) * LANES)
            v_ref[0, hd, rows, :LANES] = u[:, vsl].astype(BF16)
            v_ref[0, hd, rows, LANES:] = ones_lane


def _proj1_call(x, mod, mod_row, tables, w, tm):
    b, n, _ = x.shape
    tab = pl.BlockSpec((tm, LANES), lambda bi, i: (i, 0))
    head = pl.BlockSpec((1, DIFF_HEADS, tm, LANES), lambda bi, i: (bi, 0, i, 0))
    return pl.pallas_call(
        _proj1_kernel,
        grid=(b, n // tm),
        in_specs=[pl.BlockSpec((1, tm, D), lambda bi, i: (bi, i, 0)),
                  pl.BlockSpec((1, 6, D), lambda bi, i: (mod_row(bi), 0, 0)),
                  tab, tab, tab,
                  pl.BlockSpec(w.shape, lambda bi, i: (0, 0))],
        out_specs=[pl.BlockSpec((1, DIFF_HEADS, 2, tm, LANES), lambda bi, i: (bi, 0, 0, i, 0)), head,
                   pl.BlockSpec((1, DIFF_HEADS, tm, 2 * LANES), lambda bi, i: (bi, 0, i, 0))],
        out_shape=[jax.ShapeDtypeStruct((b, DIFF_HEADS, 2, n, LANES), BF16),
                   jax.ShapeDtypeStruct((b, DIFF_HEADS, n, LANES), BF16),
                   jax.ShapeDtypeStruct((b, DIFF_HEADS, n, 2 * LANES), BF16)],
        compiler_params=_cparams(2),
        name="proj1",
    )(x, mod, *tables, w)


def _pad_heads(w, heads, width):
    kdim = w.shape[0]
    w3 = jnp.pad(w.reshape(kdim, heads, width), ((0, 0), (0, 0), (0, LANES - width)))
    return w3.reshape(kdim, heads * LANES)


def _layer0_weights(w_in, q_norm, w_uq, kv_norm, w_ukv):
    o3 = FOURIER_WIDTH + MLA_Q_LORA + MLA_KV_LORA
    w_kr = jnp.zeros((D, LANES), F32).at[:, MLA_NOPE:MLA_NOPE + MLA_ROPE].set(w_in[:, o3:])
    wa = jnp.concatenate([w_in[:, :o3], w_kr], axis=1).astype(BF16)
    cc, sc = _cos_sin(FOURIER_GROUP_DIM)
    dft = jnp.asarray(np.concatenate([cc, -sc], axis=1), F32).astype(BF16)
    wuq = _pad_heads(w_uq, MLA_HEADS, MLA_NOPE + MLA_ROPE).astype(BF16)
    ukv = w_ukv.reshape(MLA_KV_LORA, MLA_HEADS, MLA_NOPE + MLA_V)
    wk = _pad_heads(ukv[:, :, :MLA_NOPE].reshape(MLA_KV_LORA, -1), MLA_HEADS, MLA_NOPE).astype(BF16)
    wv = _pad_heads(ukv[:, :, MLA_NOPE:].reshape(MLA_KV_LORA, -1), MLA_HEADS, MLA_V).astype(BF16)
    return wa, dft, q_norm.reshape(1, -1), wuq, kv_norm.reshape(1, -1), wk, wv


def kernel(x, c, ctx, c_ctx,
           l0_w_mod, l0_b_mod, l0_w_in, l0_q_norm, l0_w_uq, l0_kv_norm, l0_w_ukv, l0_w_out,
           l0_ln1_g, l0_ln1_b, l0_w_gate, l0_w_up, l0_w_down, l0_ln2_g, l0_ln2_b,
           l1_w_mod, l1_b_mod, l1_w_in, l1_lambda_q1, l1_lambda_k1, l1_lambda_q2, l1_lambda_k2,
           l1_subln, l1_w_out, l1_ln1_g, l1_ln1_b, l1_w_gate, l1_w_up, l1_w_down, l1_ln2_g, l1_ln2_b):
    b, n, _ = x.shape
    nc = ctx.shape[1]
    assert b + 1 <= MOD_ROWS and n % 512 == 0 and nc % LANES == 0
    tm, tq = 512, 256
    ctx_row = b
    lat_row = lambda bi: bi
    cx_row = lambda bi: ctx_row
    row = lambda a: a.reshape(1, -1)

    c_all = jnp.zeros((MOD_ROWS, D), F32).at[:b].set(c).at[b].set(c_ctx)
    mod0 = _mod_call(c_all, l0_w_mod, l0_b_mod)
    mod1 = _mod_call(c_all, l1_w_mod, l1_b_mod)
    ident = _identity_tables(nc)

    w0 = _layer0_weights(l0_w_in, l0_q_norm, l0_w_uq, l0_kv_norm, l0_w_ukv)
    tab0 = _rope_tables(n, MLA_ROPE, [MLA_NOPE])
    u, q, k, v = _proj0_call(x, mod0, lat_row, tab0, w0, tm)
    uc, qc, kc, vc = _proj0_call(ctx, mod0, cx_row, ident, w0, nc)
    att = _mla_attn_call(q, kc, vc, k, v, tq)
    attc = _mla_ctx_attn_call(qc, kc, vc)
    four = _fourier_positions(u)
    fourc = _fourier_positions(uc)
    wo = l0_w_out.astype(BF16)
    ffn0 = (row(l0_ln1_g), row(l0_ln1_b), l0_w_gate.astype(BF16), l0_w_up.astype(BF16), l0_w_down.astype(BF16),
            row(l0_ln2_g), row(l0_ln2_b))
    x = _post_call(x, mod0, lat_row, [(four, wo[:FOURIER_WIDTH]), (att, wo[FOURIER_WIDTH:])], *ffn0, tm)
    xc = _post_call(ctx, mod0, cx_row, [(fourc, wo[:FOURIER_WIDTH]), (attc, wo[FOURIER_WIDTH:])], *ffn0, nc)

    lambda_init = 0.8 - 0.6 * math.exp(-0.3 * 1)
    w1 = l1_w_in.astype(BF16)
    tab1 = _rope_tables(n, DIFF_HEAD_DIM, [0, DIFF_HEAD_DIM])
    q, k, v = _proj1_call(x, mod1, lat_row, tab1, w1, tm)
    _, kc, vc = _proj1_call(xc, mod1, cx_row, ident, w1, nc)
    lam_params = jnp.stack([l1_lambda_q1, l1_lambda_k1, l1_lambda_q2, l1_lambda_k2])
    att = _diff_attn_call(lam_params, row(l1_subln), q, kc, vc, k, v, tq, lambda_init)
    ffn1 = (row(l1_ln1_g), row(l1_ln1_b), l1_w_gate.astype(BF16), l1_w_up.astype(BF16), l1_w_down.astype(BF16),
            row(l1_ln2_g), row(l1_ln2_b))
    return _post_call(x, mod1, lat_row, [(att, l1_w_out.astype(BF16))], *ffn1, tm)
```

```python
import functools
import math

import numpy as np
import jax
import jax.numpy as jnp
from jax import lax
from jax.experimental import pallas as pl
from jax.experimental.pallas import tpu as pltpu

F32 = jnp.float32
BF16 = jnp.bfloat16

D = 1024
DEPTH = 2
GRID_W = 64
ROPE_BASE = 10000.0
LN_EPS = 1e-6
RMS_EPS = 1e-6
DN_ALPHA = (2 * DEPTH) ** 0.25
LANES = 128
FOURIER_GROUPS = 4
FOURIER_GROUP_DIM = 128
FOURIER_WIDTH = 512
MLA_HEADS = 8
MLA_Q_LORA = 256
MLA_KV_LORA = 256
MLA_NOPE = 64
MLA_ROPE = 32
MLA_V = 64
MLA_SCALE = (MLA_NOPE + MLA_ROPE) ** -0.5
DIFF_HEADS = 8
DIFF_HEAD_DIM = 64
DIFF_SCALE = DIFF_HEAD_DIM ** -0.5
LOG2E = math.log2(math.e)
MXU_TILE = 256
FF_HIDDEN = 2816
FF_CHUNKS = (1536, 1280)
ROW_SUBTILES = 2
MOD_ROWS = 8
DFT_N1 = 128
VMEM_LIMIT = 56 * 2 ** 20


def _cparams(n_axes):
    return pltpu.CompilerParams(dimension_semantics=("arbitrary",) * n_axes, vmem_limit_bytes=VMEM_LIMIT)


def _layer_norm(x):
    mu = jnp.mean(x, axis=-1, keepdims=True)
    xc = x - mu
    var = jnp.mean(xc * xc, axis=-1, keepdims=True)
    return xc * lax.rsqrt(var + LN_EPS)


def _rms(x):
    return x * lax.rsqrt(jnp.mean(x * x, axis=-1, keepdims=True) + RMS_EPS)


def _silu(x):
    return x * (1.0 / (1.0 + jnp.exp(-x)))


def _rope(x, cos, sa, sb, q):
    return x * cos + pltpu.roll(x, LANES - q, 1) * sa + pltpu.roll(x, q, 1) * sb


def _row_subtiles(tm):
    n_sub = ROW_SUBTILES if tm % (ROW_SUBTILES * MXU_TILE) == 0 else 1
    return [slice(r * (tm // n_sub), (r + 1) * (tm // n_sub)) for r in range(n_sub)]


def _dot(a, b):
    return jnp.dot(a, b, preferred_element_type=F32)


def _dot_nt(a, b):
    return lax.dot_general(a, b, (((1,), (1,)), ((), ())), preferred_element_type=F32)


def _mod_kernel(c_ref, w_ref, b_ref, o_ref):
    o_ref[...] = _dot(_silu(c_ref[...]), w_ref[...]) + b_ref[...]


def _mod_call(c_all, w, b):
    n_out = w.shape[1]
    tn = 1024
    return pl.pallas_call(
        _mod_kernel,
        grid=(n_out // tn,),
        in_specs=[pl.BlockSpec((MOD_ROWS, D), lambda j: (0, 0)),
                  pl.BlockSpec((D, tn), lambda j: (0, j)),
                  pl.BlockSpec((1, tn), lambda j: (0, j))],
        out_specs=pl.BlockSpec((MOD_ROWS, tn), lambda j: (0, j)),
        out_shape=jax.ShapeDtypeStruct((MOD_ROWS, n_out), F32),
        compiler_params=_cparams(1),
        name="adaln_mod",
    )(c_all, w, b.reshape(1, n_out)).reshape(MOD_ROWS, 6, D)


def _rope_tables(n, rope_dim, regions):
    half, quarter = rope_dim // 2, rope_dim // 4
    is_rope = np.zeros(LANES, bool)
    use_col = np.zeros(LANES, bool)
    is_x1 = np.zeros(LANES, bool)
    fi = np.zeros(LANES, np.int32)
    for off in regions:
        for r in range(rope_dim):
            j = off + r
            rr = r % half
            is_rope[j] = True
            use_col[j] = r >= half
            is_x1[j] = rr < quarter
            fi[j] = rr % quarter
    t = jnp.arange(n, dtype=jnp.int32)
    rows = (t // GRID_W).astype(F32)[:, None]
    cols = (t % GRID_W).astype(F32)[:, None]
    inv = 1.0 / (ROPE_BASE ** (jnp.arange(quarter, dtype=F32) / quarter))
    ang = jnp.where(use_col[None, :], cols, rows) * inv[fi][None, :]
    cos = jnp.where(is_rope[None, :], jnp.cos(ang), 1.0)
    sin = jnp.sin(ang)
    sa = jnp.where((is_rope & is_x1)[None, :], -sin, 0.0)
    sb = jnp.where((is_rope & ~is_x1)[None, :], sin, 0.0)
    return cos, sa, sb


def _identity_tables(n):
    return jnp.ones((n, LANES), F32), jnp.zeros((n, LANES), F32), jnp.zeros((n, LANES), F32)


def _proj0_kernel(x_ref, mod_ref, cos_ref, sa_ref, sb_ref, wa_ref, dft_ref, qn_ref, wuq_ref, kvn_ref,
                  wk_ref, wv_ref, u_ref, q_ref, k_ref, v_ref):
    m = mod_ref[0]
    sub = _row_subtiles(x_ref.shape[1])
    hs = [(_layer_norm(x_ref[0, rows]) * (1.0 + m[1:2]) + m[0:1]).astype(BF16) for rows in sub]
    us = [_dot(h, wa_ref[...]) for h in hs]
    o1 = FOURIER_WIDTH
    o2 = o1 + MLA_Q_LORA
    o3 = o2 + MLA_KV_LORA
    for rows, u in zip(sub, us):
        for g in range(FOURIER_GROUPS):
            lo = g * FOURIER_GROUP_DIM
            z = _dot(u[:, lo:lo + FOURIER_GROUP_DIM].astype(BF16), dft_ref[...])
            u_ref[0, 0, rows, lo:lo + FOURIER_GROUP_DIM] = z[:, :FOURIER_GROUP_DIM].astype(BF16)
            u_ref[0, 1, rows, lo:lo + FOURIER_GROUP_DIM] = z[:, FOURIER_GROUP_DIM:].astype(BF16)
    cqs = [(_rms(u[:, o1:o2]) * qn_ref[...]).astype(BF16) for u in us]
    ckvs = [(_rms(u[:, o2:o3]) * kvn_ref[...]).astype(BF16) for u in us]
    qs = [_dot(cq, wuq_ref[...]) for cq in cqs]
    kns = [_dot(ckv, wk_ref[...]) for ckv in ckvs]
    vs = [_dot(ckv, wv_ref[...]) for ckv in ckvs]
    for rows, u, q, kn, v in zip(sub, us, qs, kns, vs):
        cos, sa, sb = cos_ref[rows], sa_ref[rows], sb_ref[rows]
        kr = _rope(u[:, o3:o3 + LANES], cos, sa, sb, MLA_ROPE // 4)
        ones_lane = (lax.broadcasted_iota(jnp.int32, (u.shape[0], LANES), 1) == MLA_V).astype(F32)
        for hd in range(MLA_HEADS):
            sl = slice(hd * LANES, (hd + 1) * LANES)
            q_ref[0, hd, rows] = (_rope(q[:, sl], cos, sa, sb, MLA_ROPE // 4) * (MLA_SCALE * LOG2E)).astype(BF16)
            k_ref[0, hd, rows] = (kn[:, sl] + kr).astype(BF16)
            v_ref[0, hd, rows] = (v[:, sl] + ones_lane).astype(BF16)


def _proj0_call(x, mod, mod_row, tables, w, tm):
    b, n, _ = x.shape
    wa, dft, qn, wuq, kvn, wk, wv = w
    const = lambda shape: pl.BlockSpec(shape, lambda bi, i: (0,) * len(shape))
    tab = pl.BlockSpec((tm, LANES), lambda bi, i: (i, 0))
    head = pl.BlockSpec((1, MLA_HEADS, tm, LANES), lambda bi, i: (bi, 0, i, 0))
    return pl.pallas_call(
        _proj0_kernel,
        grid=(b, n // tm),
        in_specs=[pl.BlockSpec((1, tm, D), lambda bi, i: (bi, i, 0)),
                  pl.BlockSpec((1, 6, D), lambda bi, i: (mod_row(bi), 0, 0)),
                  tab, tab, tab,
                  const(wa.shape), const(dft.shape), const(qn.shape), const(wuq.shape), const(kvn.shape),
                  const(wk.shape), const(wv.shape)],
        out_specs=[pl.BlockSpec((1, 2, tm, FOURIER_WIDTH), lambda bi, i: (bi, 0, i, 0)), head, head, head],
        out_shape=[jax.ShapeDtypeStruct((b, 2, n, FOURIER_WIDTH), BF16)]
        + [jax.ShapeDtypeStruct((b, MLA_HEADS, n, LANES), BF16)] * 3,
        compiler_params=_cparams(2),
        name="proj0",
    )(x, mod, *tables, wa, dft, qn, wuq, kvn, wk, wv)


def _softmax_pv(q, kc, vc, k, v):
    s_c = _dot_nt(q, kc)
    m = jnp.max(s_c, axis=-1, keepdims=True)
    if k is not None:
        s_l = _dot_nt(q, k)
        m = jnp.maximum(m, jnp.max(s_l, axis=-1, keepdims=True))
    acc = _dot(jnp.exp2(s_c - m).astype(BF16), vc)
    if k is not None:
        acc = acc + _dot(jnp.exp2(s_l - m).astype(BF16), v)
    return acc


def _merge_head_pair(o_even, o_odd):
    lane = lax.broadcasted_iota(jnp.int32, o_even.shape, 1)
    return jnp.where(lane < MLA_V, o_even, pltpu.roll(o_odd, MLA_V, 1))


def _mla_ctx_attn_kernel(q_ref, kc_ref, vc_ref, o_ref):
    outs = []
    for hh in range(2):
        acc = _softmax_pv(q_ref[0, hh], kc_ref[0, hh], vc_ref[0, hh], None, None)
        outs.append(acc / acc[:, MLA_V:MLA_V + 1])
    o_ref[0] = _merge_head_pair(*outs).astype(BF16)


def _mla_ctx_attn_call(q, kc, vc):
    b, h, nc, _ = q.shape
    pair = pl.BlockSpec((1, 2, nc, LANES), lambda bi, hp: (bi, hp, 0, 0))
    return pl.pallas_call(
        _mla_ctx_attn_kernel,
        grid=(b, h // 2),
        in_specs=[pair, pair, pair],
        out_specs=pl.BlockSpec((1, nc, LANES), lambda bi, hp: (bi, 0, hp)),
        out_shape=jax.ShapeDtypeStruct((b, nc, h // 2 * LANES), BF16),
        compiler_params=_cparams(2),
        name="mla_ctx_attn",
    )(q, kc, vc)


def _scores_into(q, kc, k, s_ref, m_ref, slot):
    nc = kc.shape[0]
    s_c = _dot_nt(q, kc)
    s_l = _dot_nt(q, k)
    s_ref[slot, :, :nc] = s_c
    s_ref[slot, :, nc:] = s_l
    m_ref[slot] = jnp.maximum(jnp.max(s_c, axis=-1, keepdims=True), jnp.max(s_l, axis=-1, keepdims=True))


def _values_from(s_ref, m_ref, slot, vc, v):
    nc = vc.shape[0]
    m = m_ref[slot]
    return (_dot(jnp.exp2(s_ref[slot, :, :nc] - m).astype(BF16), vc)
            + _dot(jnp.exp2(s_ref[slot, :, nc:] - m).astype(BF16), v))


def _init_pipeline(scratch_refs):
    @pl.when(pl.program_id(0) == 0)
    def _():
        for r in scratch_refs:
            r[...] = jnp.zeros(r.shape, r.dtype)


def _mla_attn_kernel(q_ref, kc_ref, k_ref, vcp_ref, vp_ref, vce_ref, ve_ref, vco_ref, vo_ref, oe_ref, oo_ref,
                     s_ref, m_ref, half_ref):
    _init_pipeline((s_ref, m_ref, half_ref))
    tq = oe_ref.shape[1]
    ta, tb = slice(0, tq), slice(tq, 2 * tq)

    def head_out(slot, vc_ref, v_ref):
        acc = _values_from(s_ref, m_ref, slot, vc_ref[0, 0], v_ref[0, 0])
        return acc / acc[:, MLA_V:MLA_V + 1]

    _scores_into(q_ref[0, 0, ta], kc_ref[0, 0], k_ref[0, 0], s_ref, m_ref, 0)
    oo_ref[0] = _merge_head_pair(half_ref[...], head_out(1, vcp_ref, vp_ref)).astype(BF16)
    _scores_into(q_ref[0, 1, ta], kc_ref[0, 1], k_ref[0, 1], s_ref, m_ref, 1)
    even_a = head_out(0, vce_ref, ve_ref)
    _scores_into(q_ref[0, 0, tb], kc_ref[0, 0], k_ref[0, 0], s_ref, m_ref, 0)
    oe_ref[0] = _merge_head_pair(even_a, head_out(1, vco_ref, vo_ref)).astype(BF16)
    _scores_into(q_ref[0, 1, tb], kc_ref[0, 1], k_ref[0, 1], s_ref, m_ref, 1)
    half_ref[...] = head_out(0, vce_ref, ve_ref)


def _tile_maps(n_b, n_h, n_q):
    last = n_b * n_h * n_q - 1

    def unravel(t):
        return t // (n_h * n_q), (t // n_q) % n_h, t % n_q

    cur = lambda t: unravel(jnp.minimum(t, last))
    prev = lambda t: unravel(jnp.maximum(t - 1, 0))
    return last + 2, cur, prev


def _mla_attn_call(q, kc, vc, k, v, tq):
    b, h, n, _ = q.shape
    nc = kc.shape[2]
    steps, cur, prev = _tile_maps(b, h // 2, n // (2 * tq))

    def pair(rows, blk):
        def index(t):
            bi, hp, j = cur(t)
            return (bi, hp, j if blk else 0, 0)
        return pl.BlockSpec((1, 2, rows, LANES), index)

    def one(rows, which, parity):
        def index(t):
            bi, hp, _ = which(t)
            return (bi, 2 * hp + parity, 0, 0)
        return pl.BlockSpec((1, 1, rows, LANES), index)

    def out_spec(which):
        def index(t):
            bi, hp, j = which(t)
            return (bi, j, hp)
        return pl.BlockSpec((1, tq, LANES), index)

    half = jax.ShapeDtypeStruct((b, n // 2, h // 2 * LANES), BF16)
    return pl.pallas_call(
        _mla_attn_kernel,
        grid=(steps,),
        in_specs=[pair(2 * tq, True), pair(nc, False), pair(n, False),
                  one(nc, prev, 1), one(n, prev, 1), one(nc, cur, 0), one(n, cur, 0), one(nc, cur, 1), one(n, cur, 1)],
        out_specs=[out_spec(cur), out_spec(prev)],
        out_shape=[half, half],
        scratch_shapes=[pltpu.VMEM((2, tq, nc + n), F32), pltpu.VMEM((2, tq, 1), F32),
                        pltpu.VMEM((tq, LANES), F32)],
        compiler_params=_cparams(1),
        name="mla_attn",
    )(q, kc, k, vc, v, vc, v, vc, v)


def _diff_attn_kernel(lam_ref, sub_ref, q_ref, kc_ref, k_ref, vcp_ref, vp_ref, vcc_ref, vcur_ref, oe_ref, oo_ref,
                      s_ref, m_ref, o0_ref, *, lambda_init):
    _init_pipeline((s_ref, m_ref, o0_ref))
    lp = lam_ref[...]
    lam = (jnp.exp(jnp.sum(lp[0:1] * lp[1:2], axis=-1, keepdims=True))
           - jnp.exp(jnp.sum(lp[2:3] * lp[3:4], axis=-1, keepdims=True)) + lambda_init)
    kc, k = kc_ref[0, 0], k_ref[0, 0]
    tq = oe_ref.shape[1]
    ta, tb = slice(0, tq), slice(tq, 2 * tq)

    def map_out(slot, vc_ref, v_ref):
        acc = _values_from(s_ref, m_ref, slot, vc_ref[0, 0], v_ref[0, 0])
        return acc[:, :LANES] / acc[:, LANES:LANES + 1]

    def finish(o0, o1):
        return (_rms(o0 - lam * o1) * sub_ref[...] * (1.0 - lambda_init)).astype(BF16)

    _scores_into(q_ref[0, 0, 0, ta], kc, k, s_ref, m_ref, 0)
    oo_ref[0] = finish(o0_ref[...], map_out(1, vcp_ref, vp_ref))
    _scores_into(q_ref[0, 0, 1, ta], kc, k, s_ref, m_ref, 1)
    o0_a = map_out(0, vcc_ref, vcur_ref)
    _scores_into(q_ref[0, 0, 0, tb], kc, k, s_ref, m_ref, 0)
    oe_ref[0] = finish(o0_a, map_out(1, vcc_ref, vcur_ref))
    _scores_into(q_ref[0, 0, 1, tb], kc, k, s_ref, m_ref, 1)
    o0_ref[...] = map_out(0, vcc_ref, vcur_ref)


def _diff_attn_call(lam_params, subln, q, kc, vc, k, v, tq, lambda_init):
    b, h, _, n, _ = q.shape
    nc = kc.shape[2]
    steps, cur, prev = _tile_maps(b, h, n // (2 * tq))

    def head(rows, width, which):
        def index(t):
            bi, hd, _ = which(t)
            return (bi, hd, 0, 0)
        return pl.BlockSpec((1, 1, rows, width), index)

    def q_index(t):
        bi, hd, j = cur(t)
        return (bi, hd, 0, j, 0)

    def out_spec(which):
        def index(t):
            bi, hd, j = which(t)
            return (bi, j, hd)
        return pl.BlockSpec((1, tq, LANES), index)

    half = jax.ShapeDtypeStruct((b, n // 2, h * LANES), BF16)
    return pl.pallas_call(
        functools.partial(_diff_attn_kernel, lambda_init=lambda_init),
        grid=(steps,),
        in_specs=[pl.BlockSpec((4, DIFF_HEAD_DIM), lambda t: (0, 0)),
                  pl.BlockSpec((1, LANES), lambda t: (0, 0)),
                  pl.BlockSpec((1, 1, 2, 2 * tq, LANES), q_index),
                  head(nc, LANES, cur), head(n, LANES, cur),
                  head(nc, 2 * LANES, prev), head(n, 2 * LANES, prev),
                  head(nc, 2 * LANES, cur), head(n, 2 * LANES, cur)],
        out_specs=[out_spec(cur), out_spec(prev)],
        out_shape=[half, half],
        scratch_shapes=[pltpu.VMEM((2, tq, nc + n), F32), pltpu.VMEM((2, tq, 1), F32),
                        pltpu.VMEM((tq, LANES), F32)],
        compiler_params=_cparams(1),
        name="diff_attn",
    )(lam_params, subln, q, kc, k, vc, v, vc, v)


def _dft_stage1_kernel(u_ref, f_ref, twc_ref, tws_ref, v_ref, *, tt2):
    f = f_ref[...]
    r = _dot(f[:, :DFT_N1], u_ref[0, 0]) + _dot(f[:, DFT_N1:], u_ref[0, 1])
    vr, vi = r[:DFT_N1], r[DFT_N1:]
    twc, tws = twc_ref[0], tws_ref[0]
    for jj in range(tt2):
        c, s = twc[:, jj:jj + 1], tws[:, jj:jj + 1]
        sl = slice(jj * FOURIER_WIDTH, (jj + 1) * FOURIER_WIDTH)
        v_ref[0, 0, jj] = (vr[:, sl] * c + vi[:, sl] * s).astype(BF16)
        v_ref[0, 1, jj] = (vi[:, sl] * c - vr[:, sl] * s).astype(BF16)


def _dft_real_kernel(m_ref, v_ref, o_ref, *, kdim):
    mat = m_ref[...]
    o_ref[0] = (_dot(mat[:, :kdim], v_ref[0, 0]) + _dot(mat[:, kdim:], v_ref[0, 1])).astype(BF16)


def _dft_real_call(mat, v, tc):
    b, _, kdim, c = v.shape
    mrows = mat.shape[0]
    return pl.pallas_call(
        functools.partial(_dft_real_kernel, kdim=kdim),
        grid=(b, c // tc),
        in_specs=[pl.BlockSpec(mat.shape, lambda bi, j: (0, 0)),
                  pl.BlockSpec((1, 2, kdim, tc), lambda bi, j: (bi, 0, 0, j))],
        out_specs=pl.BlockSpec((1, mrows, tc), lambda bi, j: (bi, 0, j)),
        out_shape=jax.ShapeDtypeStruct((b, mrows, c), BF16),
        compiler_params=_cparams(2),
        name="dft_real",
    )(mat, v)


def _cos_sin(n):
    idx = np.arange(n, dtype=np.int64)
    ang = 2.0 * np.pi * ((idx[:, None] * idx[None, :]) % n) / n
    return np.cos(ang), np.sin(ang)


def _fourier_positions(u):
    b, _, n, c = u.shape
    scale = 1.0 / math.sqrt(n * FOURIER_GROUP_DIM)
    if n <= 256:
        cm, sm = _cos_sin(n)
        mat = jnp.asarray(np.concatenate([cm, sm], axis=1) * scale, F32).astype(BF16)
        return _dft_real_call(mat, u, c)
    n1, n2 = DFT_N1, n // DFT_N1
    c1, s1 = _cos_sin(n1)
    f1 = jnp.asarray(np.block([[c1, s1], [-s1, c1]]), F32).astype(BF16)
    k1 = np.arange(n1, dtype=np.int64)[:, None]
    t2 = np.arange(n2, dtype=np.int64)[None, :]
    ang = 2.0 * np.pi * ((k1 * t2) % n) / n
    tt2 = 4
    tw_shape = lambda a: jnp.asarray(a.reshape(n1, n2 // tt2, tt2).transpose(1, 0, 2), F32)
    twc, tws = tw_shape(np.cos(ang)), tw_shape(np.sin(ang))
    v = pl.pallas_call(
        functools.partial(_dft_stage1_kernel, tt2=tt2),
        grid=(n2 // tt2, b),
        in_specs=[pl.BlockSpec((1, 2, n1, tt2 * c), lambda j, bi: (bi, 0, 0, j)),
                  pl.BlockSpec((2 * n1, 2 * n1), lambda j, bi: (0, 0)),
                  pl.BlockSpec((1, n1, tt2), lambda j, bi: (j, 0, 0)),
                  pl.BlockSpec((1, n1, tt2), lambda j, bi: (j, 0, 0))],
        out_specs=pl.BlockSpec((1, 2, tt2, n1, c), lambda j, bi: (bi, 0, j, 0, 0)),
        out_shape=jax.ShapeDtypeStruct((b, 2, n2, n1, c), BF16),
        compiler_params=_cparams(2),
        name="dft_stage1",
    )(u.reshape(b, 2, n1, n2 * c), f1, twc, tws)
    c2, s2 = _cos_sin(n2)
    mat = jnp.asarray(np.concatenate([c2, s2], axis=1) * scale, F32).astype(BF16)
    y = _dft_real_call(mat, v.reshape(b, 2, n2, n1 * c), 8192)
    return y.reshape(b, n, c)


def _post_kernel(*refs, pieces):
    x_ref, mod_ref = refs[:2]
    mix, pos = [], 2
    for n_pieces in pieces:
        mix.append((refs[pos:pos + n_pieces], refs[pos + n_pieces]))
        pos += n_pieces + 1
    g1_ref, b1_ref, wg_ref, wu_ref, wd_ref, g2_ref, b2_ref, o_ref = refs[pos:]
    m = mod_ref[0]
    sub = _row_subtiles(x_ref.shape[1])
    x1s, hs, fs = [], [], []
    for r, rows in enumerate(sub):
        y = None
        for parts, w_ref in mix:
            a = parts[0][0, rows] if len(parts) == 1 else parts[r][0]
            t = _dot(a, w_ref[...])
            y = t if y is None else y + t
        x1 = _layer_norm(DN_ALPHA * x_ref[0, rows] + m[2:3] * y) * g1_ref[...] + b1_ref[...]
        x1s.append(x1)
        hs.append((_layer_norm(x1) * (1.0 + m[4:5]) + m[3:4]).astype(BF16))
    for h in hs:
        f = None
        lo = 0
        for width in FF_CHUNKS:
            sl = slice(lo, lo + width)
            lo += width
            act = _silu(_dot(h, wg_ref[:, sl])) * _dot(h, wu_ref[:, sl])
            t = _dot(act.astype(BF16), wd_ref[sl, :])
            f = t if f is None else f + t
        fs.append(f)
    for rows, x1, f in zip(sub, x1s, fs):
        o_ref[0, rows] = _layer_norm(DN_ALPHA * x1 + m[5:6] * f) * g2_ref[...] + b2_ref[...]


def _post_call(x, mod, mod_row, mixes, g1, b1, wg, wu, wd, g2, b2, tm):
    b, n, _ = x.shape
    const = lambda a: pl.BlockSpec(a.shape, lambda bi, i: (0,) * a.ndim, pipeline_mode=pl.Buffered(1))
    in_specs = [pl.BlockSpec((1, tm, D), lambda bi, i: (bi, i, 0)),
                pl.BlockSpec((1, 6, D), lambda bi, i: (mod_row(bi), 0, 0))]
    args = [x, mod]
    for parts, w in mixes:
        assert len(parts) in (1, len(_row_subtiles(tm)))
        for a in parts:
            in_specs.append(pl.BlockSpec((1, tm // len(parts), a.shape[2]), lambda bi, i: (bi, i, 0)))
        in_specs.append(const(w))
        args += [*parts, w]
    tail = [g1, b1, wg, wu, wd, g2, b2]
    in_specs += [const(a) for a in tail]
    return pl.pallas_call(
        functools.partial(_post_kernel, pieces=tuple(len(parts) for parts, _ in mixes)),
        grid=(b, n // tm),
        in_specs=in_specs,
        out_specs=pl.BlockSpec((1, tm, D), lambda bi, i: (bi, i, 0)),
        out_shape=jax.ShapeDtypeStruct((b, n, D), F32),
        compiler_params=_cparams(2),
        name="post",
    )(*args, *tail)


def _proj1_kernel(x_ref, mod_ref, cos_ref, sa_ref, sb_ref, w_ref, q_ref, k_ref, v_ref):
    m = mod_ref[0]
    sub = _row_subtiles(x_ref.shape[1])
    hs = [(_layer_norm(x_ref[0, rows]) * (1.0 + m[1:2]) + m[0:1]).astype(BF16) for rows in sub]
    us = [_dot(h, w_ref[...]) for h in hs]
    width = DIFF_HEADS * LANES
    for rows, u in zip(sub, us):
        cos, sa, sb = cos_ref[rows], sa_ref[rows], sb_ref[rows]
        lane = lax.broadcasted_iota(jnp.int32, (u.shape[0], LANES), 1)
        first = lane < DIFF_HEAD_DIM
        ones_lane = (lane == 0).astype(BF16)
        for hd in range(DIFF_HEADS):
            sl = slice(hd * LANES, (hd + 1) * LANES)
            q = _rope(u[:, sl], cos, sa, sb, DIFF_HEAD_DIM // 4) * (DIFF_SCALE * LOG2E)
            q_ref[0, hd, 0, rows] = jnp.where(first, q, 0.0).astype(BF16)
            q_ref[0, hd, 1, rows] = jnp.where(first, 0.0, q).astype(BF16)
            ksl = slice(width + hd * LANES, width + (hd + 1) * LANES)
            k_ref[0, hd, rows] = _rope(u[:, ksl], cos, sa, sb, DIFF_HEAD_DIM // 4).astype(BF16)
            vsl = slice(2 * width + hd * LANES, 2 * width + (hd + 1) * LANES)
            v_ref[0, hd, rows, :LANES] = u[:, vsl].astype(BF16)
            v_ref[0, hd, rows, LANES:] = ones_lane


def _proj1_call(x, mod, mod_row, tables, w, tm):
    b, n, _ = x.shape
    tab = pl.BlockSpec((tm, LANES), lambda bi, i: (i, 0))
    head = pl.BlockSpec((1, DIFF_HEADS, tm, LANES), lambda bi, i: (bi, 0, i, 0))
    return pl.pallas_call(
        _proj1_kernel,
        grid=(b, n // tm),
        in_specs=[pl.BlockSpec((1, tm, D), lambda bi, i: (bi, i, 0)),
                  pl.BlockSpec((1, 6, D), lambda bi, i: (mod_row(bi), 0, 0)),
                  tab, tab, tab,
                  pl.BlockSpec(w.shape, lambda bi, i: (0, 0))],
        out_specs=[pl.BlockSpec((1, DIFF_HEADS, 2, tm, LANES), lambda bi, i: (bi, 0, 0, i, 0)), head,
                   pl.BlockSpec((1, DIFF_HEADS, tm, 2 * LANES), lambda bi, i: (bi, 0, i, 0))],
        out_shape=[jax.ShapeDtypeStruct((b, DIFF_HEADS, 2, n, LANES), BF16),
                   jax.ShapeDtypeStruct((b, DIFF_HEADS, n, LANES), BF16),
                   jax.ShapeDtypeStruct((b, DIFF_HEADS, n, 2 * LANES), BF16)],
        compiler_params=_cparams(2),
        name="proj1",
    )(x, mod, *tables, w)


def _pad_heads(w, heads, width):
    kdim = w.shape[0]
    w3 = jnp.pad(w.reshape(kdim, heads, width), ((0, 0), (0, 0), (0, LANES - width)))
    return w3.reshape(kdim, heads * LANES)


def _layer0_weights(w_in, q_norm, w_uq, kv_norm, w_ukv):
    o3 = FOURIER_WIDTH + MLA_Q_LORA + MLA_KV_LORA
    w_kr = jnp.zeros((D, LANES), F32).at[:, MLA_NOPE:MLA_NOPE + MLA_ROPE].set(w_in[:, o3:])
    wa = jnp.concatenate([w_in[:, :o3], w_kr], axis=1).astype(BF16)
    cc, sc = _cos_sin(FOURIER_GROUP_DIM)
    dft = jnp.asarray(np.concatenate([cc, -sc], axis=1), F32).astype(BF16)
    wuq = _pad_heads(w_uq, MLA_HEADS, MLA_NOPE + MLA_ROPE).astype(BF16)
    ukv = w_ukv.reshape(MLA_KV_LORA, MLA_HEADS, MLA_NOPE + MLA_V)
    wk = _pad_heads(ukv[:, :, :MLA_NOPE].reshape(MLA_KV_LORA, -1), MLA_HEADS, MLA_NOPE).astype(BF16)
    wv = _pad_heads(ukv[:, :, MLA_NOPE:].reshape(MLA_KV_LORA, -1), MLA_HEADS, MLA_V).astype(BF16)
    return wa, dft, q_norm.reshape(1, -1), wuq, kv_norm.reshape(1, -1), wk, wv


def kernel(x, c, ctx, c_ctx,
           l0_w_mod, l0_b_mod, l0_w_in, l0_q_norm, l0_w_uq, l0_kv_norm, l0_w_ukv, l0_w_out,
           l0_ln1_g, l0_ln1_b, l0_w_gate, l0_w_up, l0_w_down, l0_ln2_g, l0_ln2_b,
           l1_w_mod, l1_b_mod, l1_w_in, l1_lambda_q1, l1_lambda_k1, l1_lambda_q2, l1_lambda_k2,
           l1_subln, l1_w_out, l1_ln1_g, l1_ln1_b, l1_w_gate, l1_w_up, l1_w_down, l1_ln2_g, l1_ln2_b):
    b, n, _ = x.shape
    nc = ctx.shape[1]
    assert b + 1 <= MOD_ROWS and n % 512 == 0 and nc % LANES == 0
    tm, tq = 512, 256
    ctx_row = b
    lat_row = lambda bi: bi
    cx_row = lambda bi: ctx_row
    row = lambda a: a.reshape(1, -1)

    c_all = jnp.zeros((MOD_ROWS, D), F32).at[:b].set(c).at[b].set(c_ctx)
    mod0 = _mod_call(c_all, l0_w_mod, l0_b_mod)
    mod1 = _mod_call(c_all, l1_w_mod, l1_b_mod)
    ident = _identity_tables(nc)

    w0 = _layer0_weights(l0_w_in, l0_q_norm, l0_w_uq, l0_kv_norm, l0_w_ukv)
    tab0 = _rope_tables(n, MLA_ROPE, [MLA_NOPE])
    u, q, k, v = _proj0_call(x, mod0, lat_row, tab0, w0, tm)
    uc, qc, kc, vc = _proj0_call(ctx, mod0, cx_row, ident, w0, nc)
    assert tm == 2 * tq
    att_tiles = _mla_attn_call(q, kc, vc, k, v, tq)
    attc = _mla_ctx_attn_call(qc, kc, vc)
    four = _fourier_positions(u)
    fourc = _fourier_positions(uc)
    wo = l0_w_out.astype(BF16)
    ffn0 = (row(l0_ln1_g), row(l0_ln1_b), l0_w_gate.astype(BF16), l0_w_up.astype(BF16), l0_w_down.astype(BF16),
            row(l0_ln2_g), row(l0_ln2_b))
    x = _post_call(x, mod0, lat_row, [([four], wo[:FOURIER_WIDTH]), (att_tiles, wo[FOURIER_WIDTH:])], *ffn0, tm)
    xc = _post_call(ctx, mod0, cx_row, [([fourc], wo[:FOURIER_WIDTH]), ([attc], wo[FOURIER_WIDTH:])], *ffn0, nc)

    lambda_init = 0.8 - 0.6 * math.exp(-0.3 * 1)
    w1 = l1_w_in.astype(BF16)
    tab1 = _rope_tables(n, DIFF_HEAD_DIM, [0, DIFF_HEAD_DIM])
    q, k, v = _proj1_call(x, mod1, lat_row, tab1, w1, tm)
    _, kc, vc = _proj1_call(xc, mod1, cx_row, ident, w1, nc)
    lam_params = jnp.stack([l1_lambda_q1, l1_lambda_k1, l1_lambda_q2, l1_lambda_k2])
    att_tiles = _diff_attn_call(lam_params, row(l1_subln), q, kc, vc, k, v, tq, lambda_init)
    ffn1 = (row(l1_ln1_g), row(l1_ln1_b), l1_w_gate.astype(BF16), l1_w_up.astype(BF16), l1_w_down.astype(BF16),
            row(l1_ln2_g), row(l1_ln2_b))
    return _post_call(x, mod1, lat_row, [(att_tiles, l1_w_out.astype(BF16))], *ffn1, tm)
```

```python
import functools
import math

import numpy as np
import jax
import jax.numpy as jnp
from jax import lax
from jax.experimental import pallas as pl
from jax.experimental.pallas import tpu as pltpu

F32 = jnp.float32
BF16 = jnp.bfloat16

D = 1024
DEPTH = 2
GRID_W = 64
ROPE_BASE = 10000.0
LN_EPS = 1e-6
RMS_EPS = 1e-6
DN_ALPHA = (2 * DEPTH) ** 0.25
LANES = 128
FOURIER_GROUPS = 4
FOURIER_GROUP_DIM = 128
FOURIER_WIDTH = 512
MLA_HEADS = 8
MLA_Q_LORA = 256
MLA_KV_LORA = 256
MLA_NOPE = 64
MLA_ROPE = 32
MLA_V = 64
MLA_SCALE = (MLA_NOPE + MLA_ROPE) ** -0.5
DIFF_HEADS = 8
DIFF_HEAD_DIM = 64
DIFF_SCALE = DIFF_HEAD_DIM ** -0.5
LOG2E = math.log2(math.e)
MXU_TILE = 256
FF_HIDDEN = 2816
FF_CHUNKS = (1536, 1280)
ROW_SUBTILES = 2
MOD_ROWS = 8
DFT_N1 = 128
ATTN_TILES = 4
VMEM_LIMIT = 56 * 2 ** 20


def _cparams(n_axes):
    return pltpu.CompilerParams(dimension_semantics=("arbitrary",) * n_axes, vmem_limit_bytes=VMEM_LIMIT)


def _layer_norm(x):
    mu = jnp.mean(x, axis=-1, keepdims=True)
    xc = x - mu
    var = jnp.mean(xc * xc, axis=-1, keepdims=True)
    return xc * lax.rsqrt(var + LN_EPS)


def _rms(x):
    return x * lax.rsqrt(jnp.mean(x * x, axis=-1, keepdims=True) + RMS_EPS)


def _silu(x):
    return x * (1.0 / (1.0 + jnp.exp(-x)))


def _rope(x, cos, sa, sb, q):
    return x * cos + pltpu.roll(x, LANES - q, 1) * sa + pltpu.roll(x, q, 1) * sb


def _row_subtiles(tm):
    n_sub = ROW_SUBTILES if tm % (ROW_SUBTILES * MXU_TILE) == 0 else 1
    return [slice(r * (tm // n_sub), (r + 1) * (tm // n_sub)) for r in range(n_sub)]


def _dot(a, b):
    return jnp.dot(a, b, preferred_element_type=F32)


def _dot_nt(a, b):
    return lax.dot_general(a, b, (((1,), (1,)), ((), ())), preferred_element_type=F32)


def _mod_kernel(c_ref, w_ref, b_ref, o_ref):
    o_ref[...] = _dot(_silu(c_ref[...]), w_ref[...]) + b_ref[...]


def _mod_call(c_all, w, b):
    n_out = w.shape[1]
    tn = 1024
    return pl.pallas_call(
        _mod_kernel,
        grid=(n_out // tn,),
        in_specs=[pl.BlockSpec((MOD_ROWS, D), lambda j: (0, 0)),
                  pl.BlockSpec((D, tn), lambda j: (0, j)),
                  pl.BlockSpec((1, tn), lambda j: (0, j))],
        out_specs=pl.BlockSpec((MOD_ROWS, tn), lambda j: (0, j)),
        out_shape=jax.ShapeDtypeStruct((MOD_ROWS, n_out), F32),
        compiler_params=_cparams(1),
        name="adaln_mod",
    )(c_all, w, b.reshape(1, n_out)).reshape(MOD_ROWS, 6, D)


def _rope_tables(n, rope_dim, regions):
    half, quarter = rope_dim // 2, rope_dim // 4
    is_rope = np.zeros(LANES, bool)
    use_col = np.zeros(LANES, bool)
    is_x1 = np.zeros(LANES, bool)
    fi = np.zeros(LANES, np.int32)
    for off in regions:
        for r in range(rope_dim):
            j = off + r
            rr = r % half
            is_rope[j] = True
            use_col[j] = r >= half
            is_x1[j] = rr < quarter
            fi[j] = rr % quarter
    inv = (1.0 / (ROPE_BASE ** (jnp.arange(quarter, dtype=F32) / quarter)))[fi][None, :]
    ang_r = jnp.arange(n // GRID_W, dtype=jnp.int32).astype(F32)[:, None] * inv
    ang_c = jnp.arange(GRID_W, dtype=jnp.int32).astype(F32)[:, None] * inv
    full = lambda by_row, by_col: jnp.where(use_col[None, None, :], by_col[None, :, :],
                                            by_row[:, None, :]).reshape(n, LANES)
    cos = jnp.where(is_rope[None, :], full(jnp.cos(ang_r), jnp.cos(ang_c)), 1.0)
    sin = full(jnp.sin(ang_r), jnp.sin(ang_c))
    sa = jnp.where((is_rope & is_x1)[None, :], -sin, 0.0)
    sb = jnp.where((is_rope & ~is_x1)[None, :], sin, 0.0)
    return cos, sa, sb


def _identity_tables(n):
    return jnp.ones((n, LANES), F32), jnp.zeros((n, LANES), F32), jnp.zeros((n, LANES), F32)


def _proj0_kernel(x_ref, mod_ref, cos_ref, sa_ref, sb_ref, wa_ref, dft_ref, qn_ref, wuq_ref, kvn_ref,
                  wk_ref, wv_ref, u_ref, q_ref, k_ref, v_ref):
    m = mod_ref[0]
    sub = _row_subtiles(x_ref.shape[1])
    hs = [(_layer_norm(x_ref[0, rows]) * (1.0 + m[1:2]) + m[0:1]).astype(BF16) for rows in sub]
    us = [_dot(h, wa_ref[...]) for h in hs]
    o1 = FOURIER_WIDTH
    o2 = o1 + MLA_Q_LORA
    o3 = o2 + MLA_KV_LORA
    for rows, u in zip(sub, us):
        for g in range(FOURIER_GROUPS):
            lo = g * FOURIER_GROUP_DIM
            z = _dot(u[:, lo:lo + FOURIER_GROUP_DIM].astype(BF16), dft_ref[...])
            u_ref[0, 0, rows, lo:lo + FOURIER_GROUP_DIM] = z[:, :FOURIER_GROUP_DIM].astype(BF16)
            u_ref[0, 1, rows, lo:lo + FOURIER_GROUP_DIM] = z[:, FOURIER_GROUP_DIM:].astype(BF16)
    cqs = [(_rms(u[:, o1:o2]) * qn_ref[...]).astype(BF16) for u in us]
    ckvs = [(_rms(u[:, o2:o3]) * kvn_ref[...]).astype(BF16) for u in us]
    qs = [_dot(cq, wuq_ref[...]) for cq in cqs]
    kns = [_dot(ckv, wk_ref[...]) for ckv in ckvs]
    vs = [_dot(ckv, wv_ref[...]) for ckv in ckvs]
    for rows, u, q, kn, v in zip(sub, us, qs, kns, vs):
        cos, sa, sb = cos_ref[rows], sa_ref[rows], sb_ref[rows]
        kr = _rope(u[:, o3:o3 + LANES], cos, sa, sb, MLA_ROPE // 4)
        ones_lane = (lax.broadcasted_iota(jnp.int32, (u.shape[0], LANES), 1) == MLA_V).astype(F32)
        for hd in range(MLA_HEADS):
            sl = slice(hd * LANES, (hd + 1) * LANES)
            q_ref[0, hd, rows] = (_rope(q[:, sl], cos, sa, sb, MLA_ROPE // 4) * (MLA_SCALE * LOG2E)).astype(BF16)
            k_ref[0, hd, rows] = (kn[:, sl] + kr).astype(BF16)
            v_ref[0, hd, rows] = (v[:, sl] + ones_lane).astype(BF16)


def _proj0_call(x, mod, mod_row, tables, w, tm):
    b, n, _ = x.shape
    wa, dft, qn, wuq, kvn, wk, wv = w
    const = lambda shape: pl.BlockSpec(shape, lambda bi, i: (0,) * len(shape))
    tab = pl.BlockSpec((tm, LANES), lambda bi, i: (i, 0))
    head = pl.BlockSpec((1, MLA_HEADS, tm, LANES), lambda bi, i: (bi, 0, i, 0))
    return pl.pallas_call(
        _proj0_kernel,
        grid=(b, n // tm),
        in_specs=[pl.BlockSpec((1, tm, D), lambda bi, i: (bi, i, 0)),
                  pl.BlockSpec((1, 6, D), lambda bi, i: (mod_row(bi), 0, 0)),
                  tab, tab, tab,
                  const(wa.shape), const(dft.shape), const(qn.shape), const(wuq.shape), const(kvn.shape),
                  const(wk.shape), const(wv.shape)],
        out_specs=[pl.BlockSpec((1, 2, tm, FOURIER_WIDTH), lambda bi, i: (bi, 0, i, 0)), head, head, head],
        out_shape=[jax.ShapeDtypeStruct((b, 2, n, FOURIER_WIDTH), BF16)]
        + [jax.ShapeDtypeStruct((b, MLA_HEADS, n, LANES), BF16)] * 3,
        compiler_params=_cparams(2),
        name="proj0",
    )(x, mod, *tables, wa, dft, qn, wuq, kvn, wk, wv)


def _softmax_pv(q, kc, vc, k, v):
    s_c = _dot_nt(q, kc)
    m = jnp.max(s_c, axis=-1, keepdims=True)
    if k is not None:
        s_l = _dot_nt(q, k)
        m = jnp.maximum(m, jnp.max(s_l, axis=-1, keepdims=True))
    acc = _dot(jnp.exp2(s_c - m).astype(BF16), vc)
    if k is not None:
        acc = acc + _dot(jnp.exp2(s_l - m).astype(BF16), v)
    return acc


def _merge_head_pair(o_even, o_odd):
    lane = lax.broadcasted_iota(jnp.int32, o_even.shape, 1)
    return jnp.where(lane < MLA_V, o_even, pltpu.roll(o_odd, MLA_V, 1))


def _mla_ctx_attn_kernel(q_ref, kc_ref, vc_ref, o_ref):
    outs = []
    for hh in range(2):
        acc = _softmax_pv(q_ref[0, hh], kc_ref[0, hh], vc_ref[0, hh], None, None)
        outs.append(acc / acc[:, MLA_V:MLA_V + 1])
    o_ref[0] = _merge_head_pair(*outs).astype(BF16)


def _mla_ctx_attn_call(q, kc, vc):
    b, h, nc, _ = q.shape
    pair = pl.BlockSpec((1, 2, nc, LANES), lambda bi, hp: (bi, hp, 0, 0))
    return pl.pallas_call(
        _mla_ctx_attn_kernel,
        grid=(b, h // 2),
        in_specs=[pair, pair, pair],
        out_specs=pl.BlockSpec((1, nc, LANES), lambda bi, hp: (bi, 0, hp)),
        out_shape=jax.ShapeDtypeStruct((b, nc, h // 2 * LANES), BF16),
        compiler_params=_cparams(2),
        name="mla_ctx_attn",
    )(q, kc, vc)


def _scores_into(q, kc, k, s_ref, m_ref, slot):
    nc = kc.shape[0]
    s_c = _dot_nt(q, kc)
    s_l = _dot_nt(q, k)
    s_ref[slot, :, :nc] = s_c
    s_ref[slot, :, nc:] = s_l
    m_ref[slot] = jnp.maximum(jnp.max(s_c, axis=-1, keepdims=True), jnp.max(s_l, axis=-1, keepdims=True))


def _values_from(s_ref, m_ref, slot, vc, v):
    nc = vc.shape[0]
    m = m_ref[slot]
    return (_dot(jnp.exp2(s_ref[slot, :, :nc] - m).astype(BF16), vc)
            + _dot(jnp.exp2(s_ref[slot, :, nc:] - m).astype(BF16), v))


def _init_pipeline(scratch_refs):
    @pl.when(pl.program_id(0) == 0)
    def _():
        for r in scratch_refs:
            r[...] = jnp.zeros(r.shape, r.dtype)


def _mla_attn_kernel(q_ref, kc_ref, k_ref, vcp_ref, vp_ref, vce_ref, ve_ref, vco_ref, vo_ref, o_ref,
                     s_ref, m_ref, half_ref, stash_ref):
    _init_pipeline((s_ref, m_ref, half_ref, stash_ref))
    tq = half_ref.shape[0]
    n_tiles = o_ref.shape[1] // tq
    tile = lambda i: slice(i * tq, (i + 1) * tq)

    def head_out(slot, vc_ref, v_ref):
        acc = _values_from(s_ref, m_ref, slot, vc_ref[0, 0], v_ref[0, 0])
        return acc / acc[:, MLA_V:MLA_V + 1]

    _scores_into(q_ref[0, 0, tile(0)], kc_ref[0, 0], k_ref[0, 0], s_ref, m_ref, 0)
    o_ref[0, tile(n_tiles - 1)] = _merge_head_pair(half_ref[...], head_out(1, vcp_ref, vp_ref)).astype(BF16)
    o_ref[0, :(n_tiles - 1) * tq] = stash_ref[...]
    for i in range(n_tiles):
        _scores_into(q_ref[0, 1, tile(i)], kc_ref[0, 1], k_ref[0, 1], s_ref, m_ref, 1)
        even = head_out(0, vce_ref, ve_ref)
        if i + 1 < n_tiles:
            _scores_into(q_ref[0, 0, tile(i + 1)], kc_ref[0, 0], k_ref[0, 0], s_ref, m_ref, 0)
            stash_ref[tile(i)] = _merge_head_pair(even, head_out(1, vco_ref, vo_ref)).astype(BF16)
        else:
            half_ref[...] = even


def _tile_maps(n_b, n_h, n_q):
    last = n_b * n_h * n_q - 1

    def unravel(t):
        return t // (n_h * n_q), (t // n_q) % n_h, t % n_q

    cur = lambda t: unravel(jnp.minimum(t, last))
    prev = lambda t: unravel(jnp.maximum(t - 1, 0))
    return last + 2, cur, prev


def _mla_attn_call(q, kc, vc, k, v, tq):
    b, h, n, _ = q.shape
    nc = kc.shape[2]
    rows = ATTN_TILES * tq
    steps, cur, prev = _tile_maps(b, h // 2, n // rows)

    def pair(size, blk):
        def index(t):
            bi, hp, j = cur(t)
            return (bi, hp, j if blk else 0, 0)
        return pl.BlockSpec((1, 2, size, LANES), index)

    def one(size, which, parity):
        def index(t):
            bi, hp, _ = which(t)
            return (bi, 2 * hp + parity, 0, 0)
        return pl.BlockSpec((1, 1, size, LANES), index)

    def out_index(t):
        bi, hp, j = prev(t)
        return (bi, j, hp)

    return pl.pallas_call(
        _mla_attn_kernel,
        grid=(steps,),
        in_specs=[pair(rows, True), pair(nc, False), pair(n, False),
                  one(nc, prev, 1), one(n, prev, 1), one(nc, cur, 0), one(n, cur, 0), one(nc, cur, 1), one(n, cur, 1)],
        out_specs=pl.BlockSpec((1, rows, LANES), out_index),
        out_shape=jax.ShapeDtypeStruct((b, n, h // 2 * LANES), BF16),
        scratch_shapes=[pltpu.VMEM((2, tq, nc + n), F32), pltpu.VMEM((2, tq, 1), F32),
                        pltpu.VMEM((tq, LANES), F32), pltpu.VMEM((rows - tq, LANES), BF16)],
        compiler_params=_cparams(1),
        name="mla_attn",
    )(q, kc, k, vc, v, vc, v, vc, v)


def _diff_attn_kernel(lam_ref, sub_ref, q_ref, kc_ref, k_ref, vcp_ref, vp_ref, vcc_ref, vcur_ref, o_ref,
                      s_ref, m_ref, o0_ref, stash_ref, *, lambda_init):
    _init_pipeline((s_ref, m_ref, o0_ref, stash_ref))
    lp = lam_ref[...]
    lam = (jnp.exp(jnp.sum(lp[0:1] * lp[1:2], axis=-1, keepdims=True))
           - jnp.exp(jnp.sum(lp[2:3] * lp[3:4], axis=-1, keepdims=True)) + lambda_init)
    kc, k = kc_ref[0, 0], k_ref[0, 0]
    tq = o0_ref.shape[0]
    n_tiles = o_ref.shape[1] // tq
    tile = lambda i: slice(i * tq, (i + 1) * tq)

    def map_out(slot, vc_ref, v_ref):
        acc = _values_from(s_ref, m_ref, slot, vc_ref[0, 0], v_ref[0, 0])
        return acc[:, :LANES] / acc[:, LANES:LANES + 1]

    def finish(o0, o1):
        return (_rms(o0 - lam * o1) * sub_ref[...] * (1.0 - lambda_init)).astype(BF16)

    _scores_into(q_ref[0, 0, 0, tile(0)], kc, k, s_ref, m_ref, 0)
    o_ref[0, tile(n_tiles - 1)] = finish(o0_ref[...], map_out(1, vcp_ref, vp_ref))
    o_ref[0, :(n_tiles - 1) * tq] = stash_ref[...]
    for i in range(n_tiles):
        _scores_into(q_ref[0, 0, 1, tile(i)], kc, k, s_ref, m_ref, 1)
        o0 = map_out(0, vcc_ref, vcur_ref)
        if i + 1 < n_tiles:
            _scores_into(q_ref[0, 0, 0, tile(i + 1)], kc, k, s_ref, m_ref, 0)
            stash_ref[tile(i)] = finish(o0, map_out(1, vcc_ref, vcur_ref))
        else:
            o0_ref[...] = o0


def _diff_attn_call(lam_params, subln, q, kc, vc, k, v, tq, lambda_init):
    b, h, _, n, _ = q.shape
    nc = kc.shape[2]
    rows = ATTN_TILES * tq
    steps, cur, prev = _tile_maps(b, h, n // rows)

    def head(size, width, which):
        def index(t):
            bi, hd, _ = which(t)
            return (bi, hd, 0, 0)
        return pl.BlockSpec((1, 1, size, width), index)

    def q_index(t):
        bi, hd, j = cur(t)
        return (bi, hd, 0, j, 0)

    def out_index(t):
        bi, hd, j = prev(t)
        return (bi, j, hd)

    return pl.pallas_call(
        functools.partial(_diff_attn_kernel, lambda_init=lambda_init),
        grid=(steps,),
        in_specs=[pl.BlockSpec((4, DIFF_HEAD_DIM), lambda t: (0, 0)),
                  pl.BlockSpec((1, LANES), lambda t: (0, 0)),
                  pl.BlockSpec((1, 1, 2, rows, LANES), q_index),
                  head(nc, LANES, cur), head(n, LANES, cur),
                  head(nc, 2 * LANES, prev), head(n, 2 * LANES, prev),
                  head(nc, 2 * LANES, cur), head(n, 2 * LANES, cur)],
        out_specs=pl.BlockSpec((1, rows, LANES), out_index),
        out_shape=jax.ShapeDtypeStruct((b, n, h * LANES), BF16),
        scratch_shapes=[pltpu.VMEM((2, tq, nc + n), F32), pltpu.VMEM((2, tq, 1), F32),
                        pltpu.VMEM((tq, LANES), F32), pltpu.VMEM((rows - tq, LANES), BF16)],
        compiler_params=_cparams(1),
        name="diff_attn",
    )(lam_params, subln, q, kc, k, vc, v, vc, v)


def _dft_stage1_kernel(u_ref, f_ref, twc_ref, tws_ref, v_ref, *, tt2):
    f = f_ref[...]
    r = _dot(f[:, :DFT_N1], u_ref[0, 0]) + _dot(f[:, DFT_N1:], u_ref[0, 1])
    vr, vi = r[:DFT_N1], r[DFT_N1:]
    twc, tws = twc_ref[0], tws_ref[0]
    for jj in range(tt2):
        c, s = twc[:, jj:jj + 1], tws[:, jj:jj + 1]
        sl = slice(jj * FOURIER_WIDTH, (jj + 1) * FOURIER_WIDTH)
        v_ref[0, 0, jj] = (vr[:, sl] * c + vi[:, sl] * s).astype(BF16)
        v_ref[0, 1, jj] = (vi[:, sl] * c - vr[:, sl] * s).astype(BF16)


def _dft_real_kernel(m_ref, v_ref, o_ref, *, kdim):
    mat = m_ref[...]
    o_ref[0] = (_dot(mat[:, :kdim], v_ref[0, 0]) + _dot(mat[:, kdim:], v_ref[0, 1])).astype(BF16)


def _dft_real_call(mat, v, tc):
    b, _, kdim, c = v.shape
    mrows = mat.shape[0]
    return pl.pallas_call(
        functools.partial(_dft_real_kernel, kdim=kdim),
        grid=(b, c // tc),
        in_specs=[pl.BlockSpec(mat.shape, lambda bi, j: (0, 0)),
                  pl.BlockSpec((1, 2, kdim, tc), lambda bi, j: (bi, 0, 0, j))],
        out_specs=pl.BlockSpec((1, mrows, tc), lambda bi, j: (bi, 0, j)),
        out_shape=jax.ShapeDtypeStruct((b, mrows, c), BF16),
        compiler_params=_cparams(2),
        name="dft_real",
    )(mat, v)


def _cos_sin(n):
    idx = np.arange(n, dtype=np.int64)
    ang = 2.0 * np.pi * ((idx[:, None] * idx[None, :]) % n) / n
    return np.cos(ang), np.sin(ang)


def _fourier_positions(u):
    b, _, n, c = u.shape
    scale = 1.0 / math.sqrt(n * FOURIER_GROUP_DIM)
    if n <= 256:
        cm, sm = _cos_sin(n)
        mat = jnp.asarray(np.concatenate([cm, sm], axis=1) * scale, F32).astype(BF16)
        return _dft_real_call(mat, u, c)
    n1, n2 = DFT_N1, n // DFT_N1
    c1, s1 = _cos_sin(n1)
    f1 = jnp.asarray(np.block([[c1, s1], [-s1, c1]]), F32).astype(BF16)
    k1 = np.arange(n1, dtype=np.int64)[:, None]
    t2 = np.arange(n2, dtype=np.int64)[None, :]
    ang = 2.0 * np.pi * ((k1 * t2) % n) / n
    tt2 = 4
    tw_shape = lambda a: jnp.asarray(a.reshape(n1, n2 // tt2, tt2).transpose(1, 0, 2), F32)
    twc, tws = tw_shape(np.cos(ang)), tw_shape(np.sin(ang))
    v = pl.pallas_call(
        functools.partial(_dft_stage1_kernel, tt2=tt2),
        grid=(n2 // tt2, b),
        in_specs=[pl.BlockSpec((1, 2, n1, tt2 * c), lambda j, bi: (bi, 0, 0, j)),
                  pl.BlockSpec((2 * n1, 2 * n1), lambda j, bi: (0, 0)),
                  pl.BlockSpec((1, n1, tt2), lambda j, bi: (j, 0, 0)),
                  pl.BlockSpec((1, n1, tt2), lambda j, bi: (j, 0, 0))],
        out_specs=pl.BlockSpec((1, 2, tt2, n1, c), lambda j, bi: (bi, 0, j, 0, 0)),
        out_shape=jax.ShapeDtypeStruct((b, 2, n2, n1, c), BF16),
        compiler_params=_cparams(2),
        name="dft_stage1",
    )(u.reshape(b, 2, n1, n2 * c), f1, twc, tws)
    c2, s2 = _cos_sin(n2)
    mat = jnp.asarray(np.concatenate([c2, s2], axis=1) * scale, F32).astype(BF16)
    y = _dft_real_call(mat, v.reshape(b, 2, n2, n1 * c), 8192)
    return y.reshape(b, n, c)


def _post_kernel(*refs, pieces):
    x_ref, mod_ref = refs[:2]
    mix, pos = [], 2
    for n_pieces in pieces:
        mix.append((refs[pos:pos + n_pieces], refs[pos + n_pieces]))
        pos += n_pieces + 1
    g1_ref, b1_ref, wg_ref, wu_ref, wd_ref, g2_ref, b2_ref, o_ref = refs[pos:]
    m = mod_ref[0]
    sub = _row_subtiles(x_ref.shape[1])
    x1s, hs, fs = [], [], []
    for r, rows in enumerate(sub):
        y = None
        for parts, w_ref in mix:
            a = parts[0][0, rows] if len(parts) == 1 else parts[r][0]
            t = _dot(a, w_ref[...])
            y = t if y is None else y + t
        x1 = _layer_norm(DN_ALPHA * x_ref[0, rows] + m[2:3] * y) * g1_ref[...] + b1_ref[...]
        x1s.append(x1)
        hs.append((_layer_norm(x1) * (1.0 + m[4:5]) + m[3:4]).astype(BF16))
    for h in hs:
        f = None
        lo = 0
        for width in FF_CHUNKS:
            sl = slice(lo, lo + width)
            lo += width
            act = _silu(_dot(h, wg_ref[:, sl])) * _dot(h, wu_ref[:, sl])
            t = _dot(act.astype(BF16), wd_ref[sl, :])
            f = t if f is None else f + t
        fs.append(f)
    for rows, x1, f in zip(sub, x1s, fs):
        o_ref[0, rows] = _layer_norm(DN_ALPHA * x1 + m[5:6] * f) * g2_ref[...] + b2_ref[...]


def _post_call(x, mod, mod_row, mixes, g1, b1, wg, wu, wd, g2, b2, tm):
    b, n, _ = x.shape
    const = lambda a: pl.BlockSpec(a.shape, lambda bi, i: (0,) * a.ndim, pipeline_mode=pl.Buffered(1))
    in_specs = [pl.BlockSpec((1, tm, D), lambda bi, i: (bi, i, 0)),
                pl.BlockSpec((1, 6, D), lambda bi, i: (mod_row(bi), 0, 0))]
    args = [x, mod]
    for parts, w in mixes:
        assert len(parts) in (1, len(_row_subtiles(tm)))
        for a in parts:
            in_specs.append(pl.BlockSpec((1, tm // len(parts), a.shape[2]), lambda bi, i: (bi, i, 0)))
        in_specs.append(const(w))
        args += [*parts, w]
    tail = [g1, b1, wg, wu, wd, g2, b2]
    in_specs += [const(a) for a in tail]
    return pl.pallas_call(
        functools.partial(_post_kernel, pieces=tuple(len(parts) for parts, _ in mixes)),
        grid=(b, n // tm),
        in_specs=in_specs,
        out_specs=pl.BlockSpec((1, tm, D), lambda bi, i: (bi, i, 0)),
        out_shape=jax.ShapeDtypeStruct((b, n, D), F32),
        compiler_params=_cparams(2),
        name="post",
    )(*args, *tail)


def _proj1_kernel(x_ref, mod_ref, cos_ref, sa_ref, sb_ref, w_ref, q_ref, k_ref, v_ref):
    m = mod_ref[0]
    sub = _row_subtiles(x_ref.shape[1])
    hs = [(_layer_norm(x_ref[0, rows]) * (1.0 + m[1:2]) + m[0:1]).astype(BF16) for rows in sub]
    us = [_dot(h, w_ref[...]) for h in hs]
    width = DIFF_HEADS * LANES
    for rows, u in zip(sub, us):
        cos, sa, sb = cos_ref[rows], sa_ref[rows], sb_ref[rows]
        lane = lax.broadcasted_iota(jnp.int32, (u.shape[0], LANES), 1)
        first = lane < DIFF_HEAD_DIM
        ones_lane = (lane == 0).astype(BF16)
        for hd in range(DIFF_HEADS):
            sl = slice(hd * LANES, (hd + 1) * LANES)
            q = _rope(u[:, sl], cos, sa, sb, DIFF_HEAD_DIM // 4) * (DIFF_SCALE * LOG2E)
            q_ref[0, hd, 0, rows] = jnp.where(first, q, 0.0).astype(BF16)
            q_ref[0, hd, 1, rows] = jnp.where(first, 0.0, q).astype(BF16)
            ksl = slice(width + hd * LANES, width + (hd + 1) * LANES)
            k_ref[0, hd, rows] = _rope(u[:, ksl], cos, sa, sb, DIFF_HEAD_DIM // 4).astype(BF16)
            vsl = slice(2 * width + hd * LANES, 2 * width + (hd + 1) * LANES)
            v_ref[0, hd, rows, :LANES] = u[:, vsl].astype(BF16)
            v_ref[0, hd, rows, LANES:] = ones_lane


def _proj1_call(x, mod, mod_row, tables, w, tm):
    b, n, _ = x.shape
    tab = pl.BlockSpec((tm, LANES), lambda bi, i: (i, 0))
    head = pl.BlockSpec((1, DIFF_HEADS, tm, LANES), lambda bi, i: (bi, 0, i, 0))
    return pl.pallas_call(
        _proj1_kernel,
        grid=(b, n // tm),
        in_specs=[pl.BlockSpec((1, tm, D), lambda bi, i: (bi, i, 0)),
                  pl.BlockSpec((1, 6, D), lambda bi, i: (mod_row(bi), 0, 0)),
                  tab, tab, tab,
                  pl.BlockSpec(w.shape, lambda bi, i: (0, 0))],
        out_specs=[pl.BlockSpec((1, DIFF_HEADS, 2, tm, LANES), lambda bi, i: (bi, 0, 0, i, 0)), head,
                   pl.BlockSpec((1, DIFF_HEADS, tm, 2 * LANES), lambda bi, i: (bi, 0, i, 0))],
        out_shape=[jax.ShapeDtypeStruct((b, DIFF_HEADS, 2, n, LANES), BF16),
                   jax.ShapeDtypeStruct((b, DIFF_HEADS, n, LANES), BF16),
                   jax.ShapeDtypeStruct((b, DIFF_HEADS, n, 2 * LANES), BF16)],
        compiler_params=_cparams(2),
        name="proj1",
    )(x, mod, *tables, w)


def _pad_heads(w, heads, width):
    kdim = w.shape[0]
    w3 = jnp.pad(w.reshape(kdim, heads, width), ((0, 0), (0, 0), (0, LANES - width)))
    return w3.reshape(kdim, heads * LANES)


def _layer0_weights(w_in, q_norm, w_uq, kv_norm, w_ukv):
    o3 = FOURIER_WIDTH + MLA_Q_LORA + MLA_KV_LORA
    w_kr = jnp.zeros((D, LANES), F32).at[:, MLA_NOPE:MLA_NOPE + MLA_ROPE].set(w_in[:, o3:])
    wa = jnp.concatenate([w_in[:, :o3], w_kr], axis=1).astype(BF16)
    cc, sc = _cos_sin(FOURIER_GROUP_DIM)
    dft = jnp.asarray(np.concatenate([cc, -sc], axis=1), F32).astype(BF16)
    wuq = _pad_heads(w_uq, MLA_HEADS, MLA_NOPE + MLA_ROPE).astype(BF16)
    ukv = w_ukv.reshape(MLA_KV_LORA, MLA_HEADS, MLA_NOPE + MLA_V)
    wk = _pad_heads(ukv[:, :, :MLA_NOPE].reshape(MLA_KV_LORA, -1), MLA_HEADS, MLA_NOPE).astype(BF16)
    wv = _pad_heads(ukv[:, :, MLA_NOPE:].reshape(MLA_KV_LORA, -1), MLA_HEADS, MLA_V).astype(BF16)
    return wa, dft, q_norm.reshape(1, -1), wuq, kv_norm.reshape(1, -1), wk, wv


def kernel(x, c, ctx, c_ctx,
           l0_w_mod, l0_b_mod, l0_w_in, l0_q_norm, l0_w_uq, l0_kv_norm, l0_w_ukv, l0_w_out,
           l0_ln1_g, l0_ln1_b, l0_w_gate, l0_w_up, l0_w_down, l0_ln2_g, l0_ln2_b,
           l1_w_mod, l1_b_mod, l1_w_in, l1_lambda_q1, l1_lambda_k1, l1_lambda_q2, l1_lambda_k2,
           l1_subln, l1_w_out, l1_ln1_g, l1_ln1_b, l1_w_gate, l1_w_up, l1_w_down, l1_ln2_g, l1_ln2_b):
    b, n, _ = x.shape
    nc = ctx.shape[1]
    tm, tq = 512, 256
    assert b + 1 <= MOD_ROWS and n % max(tm, ATTN_TILES * tq) == 0 and nc % LANES == 0
    ctx_row = b
    lat_row = lambda bi: bi
    cx_row = lambda bi: ctx_row
    row = lambda a: a.reshape(1, -1)

    c_all = jnp.zeros((MOD_ROWS, D), F32).at[:b].set(c).at[b].set(c_ctx)
    mod0 = _mod_call(c_all, l0_w_mod, l0_b_mod)
    mod1 = _mod_call(c_all, l1_w_mod, l1_b_mod)
    ident = _identity_tables(nc)

    w0 = _layer0_weights(l0_w_in, l0_q_norm, l0_w_uq, l0_kv_norm, l0_w_ukv)
    tab0 = _rope_tables(n, MLA_ROPE, [MLA_NOPE])
    u, q, k, v = _proj0_call(x, mod0, lat_row, tab0, w0, tm)
    uc, qc, kc, vc = _proj0_call(ctx, mod0, cx_row, ident, w0, nc)
    att = _mla_attn_call(q, kc, vc, k, v, tq)
    attc = _mla_ctx_attn_call(qc, kc, vc)
    four = _fourier_positions(u)
    fourc = _fourier_positions(uc)
    wo = l0_w_out.astype(BF16)
    ffn0 = (row(l0_ln1_g), row(l0_ln1_b), l0_w_gate.astype(BF16), l0_w_up.astype(BF16), l0_w_down.astype(BF16),
            row(l0_ln2_g), row(l0_ln2_b))
    x = _post_call(x, mod0, lat_row, [([four], wo[:FOURIER_WIDTH]), ([att], wo[FOURIER_WIDTH:])], *ffn0, tm)
    xc = _post_call(ctx, mod0, cx_row, [([fourc], wo[:FOURIER_WIDTH]), ([attc], wo[FOURIER_WIDTH:])], *ffn0, nc)

    lambda_init = 0.8 - 0.6 * math.exp(-0.3 * 1)
    w1 = l1_w_in.astype(BF16)
    tab1 = _rope_tables(n, DIFF_HEAD_DIM, [0, DIFF_HEAD_DIM])
    q, k, v = _proj1_call(x, mod1, lat_row, tab1, w1, tm)
    _, kc, vc = _proj1_call(xc, mod1, cx_row, ident, w1, nc)
    lam_params = jnp.stack([l1_lambda_q1, l1_lambda_k1, l1_lambda_q2, l1_lambda_k2])
    att = _diff_attn_call(lam_params, row(l1_subln), q, kc, vc, k, v, tq, lambda_init)
    ffn1 = (row(l1_ln1_g), row(l1_ln1_b), l1_w_gate.astype(BF16), l1_w_up.astype(BF16), l1_w_down.astype(BF16),
            row(l1_ln2_g), row(l1_ln2_b))
    return _post_call(x, mod1, lat_row, [([att], l1_w_out.astype(BF16))], *ffn1, tm)
```

```python
import functools
import math

import numpy as np
import jax
import jax.numpy as jnp
from jax import lax
from jax.experimental import pallas as pl
from jax.experimental.pallas import tpu as pltpu

F32 = jnp.float32
BF16 = jnp.bfloat16

D = 1024
DEPTH = 2
GRID_W = 64
ROPE_BASE = 10000.0
LN_EPS = 1e-6
RMS_EPS = 1e-6
DN_ALPHA = (2 * DEPTH) ** 0.25
LANES = 128
FOURIER_GROUPS = 4
FOURIER_GROUP_DIM = 128
FOURIER_WIDTH = 512
MLA_HEADS = 8
MLA_Q_LORA = 256
MLA_KV_LORA = 256
MLA_NOPE = 64
MLA_ROPE = 32
MLA_V = 64
MLA_SCALE = (MLA_NOPE + MLA_ROPE) ** -0.5
DIFF_HEADS = 8
DIFF_HEAD_DIM = 64
DIFF_SCALE = DIFF_HEAD_DIM ** -0.5
LOG2E = math.log2(math.e)
MXU_TILE = 256
FF_HIDDEN = 2816
FF_CHUNKS = (1536, 1280)
ROW_SUBTILES = 2
MOD_ROWS = 8
DFT_N1 = 128
ATTN_TILES = 4
VMEM_LIMIT = 56 * 2 ** 20


def _cparams(n_axes):
    return pltpu.CompilerParams(dimension_semantics=("arbitrary",) * n_axes, vmem_limit_bytes=VMEM_LIMIT)


def _layer_norm(x):
    mu = jnp.mean(x, axis=-1, keepdims=True)
    xc = x - mu
    var = jnp.mean(xc * xc, axis=-1, keepdims=True)
    return xc * lax.rsqrt(var + LN_EPS)


def _rms(x):
    return x * lax.rsqrt(jnp.mean(x * x, axis=-1, keepdims=True) + RMS_EPS)


def _silu(x):
    return x * (1.0 / (1.0 + jnp.exp(-x)))


def _rope(x, cos, sa, sb, q):
    return x * cos + pltpu.roll(x, LANES - q, 1) * sa + pltpu.roll(x, q, 1) * sb


def _row_subtiles(tm):
    n_sub = ROW_SUBTILES if tm % (ROW_SUBTILES * MXU_TILE) == 0 else 1
    return [slice(r * (tm // n_sub), (r + 1) * (tm // n_sub)) for r in range(n_sub)]


def _dot(a, b):
    return jnp.dot(a, b, preferred_element_type=F32)


def _dot_nt(a, b):
    return lax.dot_general(a, b, (((1,), (1,)), ((), ())), preferred_element_type=F32)


def _mod_kernel(c_ref, w_ref, b_ref, o_ref):
    o_ref[...] = _dot(_silu(c_ref[...]), w_ref[...]) + b_ref[...]


def _mod_call(c_all, w, b):
    n_out = w.shape[1]
    tn = 1024
    return pl.pallas_call(
        _mod_kernel,
        grid=(n_out // tn,),
        in_specs=[pl.BlockSpec((MOD_ROWS, D), lambda j: (0, 0)),
                  pl.BlockSpec((D, tn), lambda j: (0, j)),
                  pl.BlockSpec((1, tn), lambda j: (0, j))],
        out_specs=pl.BlockSpec((MOD_ROWS, tn), lambda j: (0, j)),
        out_shape=jax.ShapeDtypeStruct((MOD_ROWS, n_out), F32),
        compiler_params=_cparams(1),
        name="adaln_mod",
    )(c_all, w, b.reshape(1, n_out)).reshape(MOD_ROWS, 6, D)


def _rope_tables(n, rope_dim, regions):
    half, quarter = rope_dim // 2, rope_dim // 4
    is_rope = np.zeros(LANES, bool)
    use_col = np.zeros(LANES, bool)
    is_x1 = np.zeros(LANES, bool)
    fi = np.zeros(LANES, np.int32)
    for off in regions:
        for r in range(rope_dim):
            j = off + r
            rr = r % half
            is_rope[j] = True
            use_col[j] = r >= half
            is_x1[j] = rr < quarter
            fi[j] = rr % quarter
    inv = (1.0 / (ROPE_BASE ** (jnp.arange(quarter, dtype=F32) / quarter)))[fi][None, :]
    ang_r = jnp.arange(n // GRID_W, dtype=jnp.int32).astype(F32)[:, None] * inv
    ang_c = jnp.arange(GRID_W, dtype=jnp.int32).astype(F32)[:, None] * inv

    def tables(ang):
        sin = jnp.sin(ang)
        return jnp.stack([jnp.where(is_rope[None, :], jnp.cos(ang), 1.0),
                          jnp.where((is_rope & is_x1)[None, :], -sin, 0.0),
                          jnp.where((is_rope & ~is_x1)[None, :], sin, 0.0)])

    return tables(ang_r), tables(ang_c), jnp.asarray(use_col[None, :], F32)


def _identity_tables(n):
    one = lambda rows: jnp.stack([jnp.ones((rows, LANES), F32), jnp.zeros((rows, LANES), F32),
                                  jnp.zeros((rows, LANES), F32)])
    return one(n // GRID_W), one(GRID_W), jnp.zeros((1, LANES), F32)


def _rope_specs(tm):
    return [pl.BlockSpec((3, tm // GRID_W, LANES), lambda bi, i: (0, i, 0)),
            pl.BlockSpec((3, GRID_W, LANES), lambda bi, i: (0, 0, 0)),
            pl.BlockSpec((1, LANES), lambda bi, i: (0, 0))]


def _rope_tile(rtab_ref, ctab_ref, mask_ref, rows):
    g0, g1 = rows.start // GRID_W, rows.stop // GRID_W
    by_col = mask_ref[...] > 0.5
    out = []
    for t in range(3):
        r = jnp.broadcast_to(rtab_ref[t, g0:g1][:, None, :], (g1 - g0, GRID_W, LANES))
        c = jnp.concatenate([ctab_ref[t]] * (g1 - g0), axis=0)
        out.append(jnp.where(by_col, c, r.reshape(rows.stop - rows.start, LANES)))
    return out


def _proj0_kernel(x_ref, mod_ref, rtab_ref, ctab_ref, mask_ref, wa_ref, dft_ref, qn_ref, wuq_ref, kvn_ref,
                  wk_ref, wv_ref, u_ref, q_ref, k_ref, v_ref):
    m = mod_ref[0]
    sub = _row_subtiles(x_ref.shape[1])
    hs = [(_layer_norm(x_ref[0, rows]) * (1.0 + m[1:2]) + m[0:1]).astype(BF16) for rows in sub]
    us = [_dot(h, wa_ref[...]) for h in hs]
    o1 = FOURIER_WIDTH
    o2 = o1 + MLA_Q_LORA
    o3 = o2 + MLA_KV_LORA
    for rows, u in zip(sub, us):
        for g in range(FOURIER_GROUPS):
            lo = g * FOURIER_GROUP_DIM
            z = _dot(u[:, lo:lo + FOURIER_GROUP_DIM].astype(BF16), dft_ref[...])
            u_ref[0, 0, rows, lo:lo + FOURIER_GROUP_DIM] = z[:, :FOURIER_GROUP_DIM].astype(BF16)
            u_ref[0, 1, rows, lo:lo + FOURIER_GROUP_DIM] = z[:, FOURIER_GROUP_DIM:].astype(BF16)
    cqs = [(_rms(u[:, o1:o2]) * qn_ref[...]).astype(BF16) for u in us]
    ckvs = [(_rms(u[:, o2:o3]) * kvn_ref[...]).astype(BF16) for u in us]
    qs = [_dot(cq, wuq_ref[...]) for cq in cqs]
    kns = [_dot(ckv, wk_ref[...]) for ckv in ckvs]
    vs = [_dot(ckv, wv_ref[...]) for ckv in ckvs]
    for rows, u, q, kn, v in zip(sub, us, qs, kns, vs):
        cos, sa, sb = _rope_tile(rtab_ref, ctab_ref, mask_ref, rows)
        kr = _rope(u[:, o3:o3 + LANES], cos, sa, sb, MLA_ROPE // 4)
        ones_lane = (lax.broadcasted_iota(jnp.int32, (u.shape[0], LANES), 1) == MLA_V).astype(F32)
        for hd in range(MLA_HEADS):
            sl = slice(hd * LANES, (hd + 1) * LANES)
            q_ref[0, hd, rows] = (_rope(q[:, sl], cos, sa, sb, MLA_ROPE // 4) * (MLA_SCALE * LOG2E)).astype(BF16)
            k_ref[0, hd, rows] = (kn[:, sl] + kr).astype(BF16)
            v_ref[0, hd, rows] = (v[:, sl] + ones_lane).astype(BF16)


def _proj0_call(x, mod, mod_row, tables, w, tm):
    b, n, _ = x.shape
    wa, dft, qn, wuq, kvn, wk, wv = w
    const = lambda shape: pl.BlockSpec(shape, lambda bi, i: (0,) * len(shape))
    head = pl.BlockSpec((1, MLA_HEADS, tm, LANES), lambda bi, i: (bi, 0, i, 0))
    return pl.pallas_call(
        _proj0_kernel,
        grid=(b, n // tm),
        in_specs=[pl.BlockSpec((1, tm, D), lambda bi, i: (bi, i, 0)),
                  pl.BlockSpec((1, 6, D), lambda bi, i: (mod_row(bi), 0, 0)),
                  *_rope_specs(tm),
                  const(wa.shape), const(dft.shape), const(qn.shape), const(wuq.shape), const(kvn.shape),
                  const(wk.shape), const(wv.shape)],
        out_specs=[pl.BlockSpec((1, 2, tm, FOURIER_WIDTH), lambda bi, i: (bi, 0, i, 0)), head, head, head],
        out_shape=[jax.ShapeDtypeStruct((b, 2, n, FOURIER_WIDTH), BF16)]
        + [jax.ShapeDtypeStruct((b, MLA_HEADS, n, LANES), BF16)] * 3,
        compiler_params=_cparams(2),
        name="proj0",
    )(x, mod, *tables, wa, dft, qn, wuq, kvn, wk, wv)


def _softmax_pv(q, kc, vc, k, v):
    s_c = _dot_nt(q, kc)
    m = jnp.max(s_c, axis=-1, keepdims=True)
    if k is not None:
        s_l = _dot_nt(q, k)
        m = jnp.maximum(m, jnp.max(s_l, axis=-1, keepdims=True))
    acc = _dot(jnp.exp2(s_c - m).astype(BF16), vc)
    if k is not None:
        acc = acc + _dot(jnp.exp2(s_l - m).astype(BF16), v)
    return acc


def _merge_head_pair(o_even, o_odd):
    lane = lax.broadcasted_iota(jnp.int32, o_even.shape, 1)
    return jnp.where(lane < MLA_V, o_even, pltpu.roll(o_odd, MLA_V, 1))


def _mla_ctx_attn_kernel(q_ref, kc_ref, vc_ref, o_ref):
    outs = []
    for hh in range(2):
        acc = _softmax_pv(q_ref[0, hh], kc_ref[0, hh], vc_ref[0, hh], None, None)
        outs.append(acc / acc[:, MLA_V:MLA_V + 1])
    o_ref[0] = _merge_head_pair(*outs).astype(BF16)


def _mla_ctx_attn_call(q, kc, vc):
    b, h, nc, _ = q.shape
    pair = pl.BlockSpec((1, 2, nc, LANES), lambda bi, hp: (bi, hp, 0, 0))
    return pl.pallas_call(
        _mla_ctx_attn_kernel,
        grid=(b, h // 2),
        in_specs=[pair, pair, pair],
        out_specs=pl.BlockSpec((1, nc, LANES), lambda bi, hp: (bi, 0, hp)),
        out_shape=jax.ShapeDtypeStruct((b, nc, h // 2 * LANES), BF16),
        compiler_params=_cparams(2),
        name="mla_ctx_attn",
    )(q, kc, vc)


def _scores_into(q, kc, k, s_ref, m_ref, slot):
    nc = kc.shape[0]
    s_c = _dot_nt(q, kc)
    s_l = _dot_nt(q, k)
    s_ref[slot, :, :nc] = s_c
    s_ref[slot, :, nc:] = s_l
    m_ref[slot] = jnp.maximum(jnp.max(s_c, axis=-1, keepdims=True), jnp.max(s_l, axis=-1, keepdims=True))


def _values_from(s_ref, m_ref, slot, vc, v):
    nc = vc.shape[0]
    m = m_ref[slot]
    return (_dot(jnp.exp2(s_ref[slot, :, :nc] - m).astype(BF16), vc)
            + _dot(jnp.exp2(s_ref[slot, :, nc:] - m).astype(BF16), v))


def _init_pipeline(scratch_refs):
    @pl.when(pl.program_id(0) == 0)
    def _():
        for r in scratch_refs:
            r[...] = jnp.zeros(r.shape, r.dtype)


def _mla_attn_kernel(q_ref, kc_ref, k_ref, vcp_ref, vp_ref, vce_ref, ve_ref, vco_ref, vo_ref, o_ref,
                     s_ref, m_ref, half_ref, stash_ref):
    _init_pipeline((s_ref, m_ref, half_ref, stash_ref))
    tq = half_ref.shape[0]
    n_tiles = o_ref.shape[1] // tq
    tile = lambda i: slice(i * tq, (i + 1) * tq)

    def head_out(slot, vc_ref, v_ref):
        acc = _values_from(s_ref, m_ref, slot, vc_ref[0, 0], v_ref[0, 0])
        return acc / acc[:, MLA_V:MLA_V + 1]

    _scores_into(q_ref[0, 0, tile(0)], kc_ref[0, 0], k_ref[0, 0], s_ref, m_ref, 0)
    o_ref[0, tile(n_tiles - 1)] = _merge_head_pair(half_ref[...], head_out(1, vcp_ref, vp_ref)).astype(BF16)
    o_ref[0, :(n_tiles - 1) * tq] = stash_ref[...]
    for i in range(n_tiles):
        _scores_into(q_ref[0, 1, tile(i)], kc_ref[0, 1], k_ref[0, 1], s_ref, m_ref, 1)
        even = head_out(0, vce_ref, ve_ref)
        if i + 1 < n_tiles:
            _scores_into(q_ref[0, 0, tile(i + 1)], kc_ref[0, 0], k_ref[0, 0], s_ref, m_ref, 0)
            stash_ref[tile(i)] = _merge_head_pair(even, head_out(1, vco_ref, vo_ref)).astype(BF16)
        else:
            half_ref[...] = even


def _tile_maps(n_b, n_h, n_q):
    last = n_b * n_h * n_q - 1

    def unravel(t):
        return t // (n_h * n_q), (t // n_q) % n_h, t % n_q

    cur = lambda t: unravel(jnp.minimum(t, last))
    prev = lambda t: unravel(jnp.maximum(t - 1, 0))
    return last + 2, cur, prev


def _mla_attn_call(q, kc, vc, k, v, tq):
    b, h, n, _ = q.shape
    nc = kc.shape[2]
    rows = ATTN_TILES * tq
    steps, cur, prev = _tile_maps(b, h // 2, n // rows)

    def pair(size, blk):
        def index(t):
            bi, hp, j = cur(t)
            return (bi, hp, j if blk else 0, 0)
        return pl.BlockSpec((1, 2, size, LANES), index)

    def one(size, which, parity):
        def index(t):
            bi, hp, _ = which(t)
            return (bi, 2 * hp + parity, 0, 0)
        return pl.BlockSpec((1, 1, size, LANES), index)

    def out_index(t):
        bi, hp, j = prev(t)
        return (bi, j, hp)

    return pl.pallas_call(
        _mla_attn_kernel,
        grid=(steps,),
        in_specs=[pair(rows, True), pair(nc, False), pair(n, False),
                  one(nc, prev, 1), one(n, prev, 1), one(nc, cur, 0), one(n, cur, 0), one(nc, cur, 1), one(n, cur, 1)],
        out_specs=pl.BlockSpec((1, rows, LANES), out_index),
        out_shape=jax.ShapeDtypeStruct((b, n, h // 2 * LANES), BF16),
        scratch_shapes=[pltpu.VMEM((2, tq, nc + n), F32), pltpu.VMEM((2, tq, 1), F32),
                        pltpu.VMEM((tq, LANES), F32), pltpu.VMEM((rows - tq, LANES), BF16)],
        compiler_params=_cparams(1),
        name="mla_attn",
    )(q, kc, k, vc, v, vc, v, vc, v)


def _diff_attn_kernel(lam_ref, sub_ref, q_ref, kc_ref, k_ref, vcp_ref, vp_ref, vcc_ref, vcur_ref, o_ref,
                      s_ref, m_ref, o0_ref, stash_ref, *, lambda_init):
    _init_pipeline((s_ref, m_ref, o0_ref, stash_ref))
    lp = lam_ref[...]
    lam = (jnp.exp(jnp.sum(lp[0:1] * lp[1:2], axis=-1, keepdims=True))
           - jnp.exp(jnp.sum(lp[2:3] * lp[3:4], axis=-1, keepdims=True)) + lambda_init)
    kc, k = kc_ref[0, 0], k_ref[0, 0]
    tq = o0_ref.shape[0]
    n_tiles = o_ref.shape[1] // tq
    tile = lambda i: slice(i * tq, (i + 1) * tq)

    def map_out(slot, vc_ref, v_ref):
        acc = _values_from(s_ref, m_ref, slot, vc_ref[0, 0], v_ref[0, 0])
        return acc[:, :LANES] / acc[:, LANES:LANES + 1]

    def finish(o0, o1):
        return (_rms(o0 - lam * o1) * sub_ref[...] * (1.0 - lambda_init)).astype(BF16)

    _scores_into(q_ref[0, 0, 0, tile(0)], kc, k, s_ref, m_ref, 0)
    o_ref[0, tile(n_tiles - 1)] = finish(o0_ref[...], map_out(1, vcp_ref, vp_ref))
    o_ref[0, :(n_tiles - 1) * tq] = stash_ref[...]
    for i in range(n_tiles):
        _scores_into(q_ref[0, 0, 1, tile(i)], kc, k, s_ref, m_ref, 1)
        o0 = map_out(0, vcc_ref, vcur_ref)
        if i + 1 < n_tiles:
            _scores_into(q_ref[0, 0, 0, tile(i + 1)], kc, k, s_ref, m_ref, 0)
            stash_ref[tile(i)] = finish(o0, map_out(1, vcc_ref, vcur_ref))
        else:
            o0_ref[...] = o0


def _diff_attn_call(lam_params, subln, q, kc, vc, k, v, tq, lambda_init):
    b, h, _, n, _ = q.shape
    nc = kc.shape[2]
    rows = ATTN_TILES * tq
    steps, cur, prev = _tile_maps(b, h, n // rows)

    def head(size, width, which):
        def index(t):
            bi, hd, _ = which(t)
            return (bi, hd, 0, 0)
        return pl.BlockSpec((1, 1, size, width), index)

    def q_index(t):
        bi, hd, j = cur(t)
        return (bi, hd, 0, j, 0)

    def out_index(t):
        bi, hd, j = prev(t)
        return (bi, j, hd)

    return pl.pallas_call(
        functools.partial(_diff_attn_kernel, lambda_init=lambda_init),
        grid=(steps,),
        in_specs=[pl.BlockSpec((4, DIFF_HEAD_DIM), lambda t: (0, 0)),
                  pl.BlockSpec((1, LANES), lambda t: (0, 0)),
                  pl.BlockSpec((1, 1, 2, rows, LANES), q_index),
                  head(nc, LANES, cur), head(n, LANES, cur),
                  head(nc, 2 * LANES, prev), head(n, 2 * LANES, prev),
                  head(nc, 2 * LANES, cur), head(n, 2 * LANES, cur)],
        out_specs=pl.BlockSpec((1, rows, LANES), out_index),
        out_shape=jax.ShapeDtypeStruct((b, n, h * LANES), BF16),
        scratch_shapes=[pltpu.VMEM((2, tq, nc + n), F32), pltpu.VMEM((2, tq, 1), F32),
                        pltpu.VMEM((tq, LANES), F32), pltpu.VMEM((rows - tq, LANES), BF16)],
        compiler_params=_cparams(1),
        name="diff_attn",
    )(lam_params, subln, q, kc, k, vc, v, vc, v)


def _dft_stage1_kernel(u_ref, f_ref, twc_ref, tws_ref, v_ref, *, tt2):
    f = f_ref[...]
    r = _dot(f[:, :DFT_N1], u_ref[0, 0]) + _dot(f[:, DFT_N1:], u_ref[0, 1])
    vr, vi = r[:DFT_N1], r[DFT_N1:]
    twc, tws = twc_ref[0], tws_ref[0]
    for jj in range(tt2):
        c, s = twc[:, jj:jj + 1], tws[:, jj:jj + 1]
        sl = slice(jj * FOURIER_WIDTH, (jj + 1) * FOURIER_WIDTH)
        v_ref[0, 0, jj] = (vr[:, sl] * c + vi[:, sl] * s).astype(BF16)
        v_ref[0, 1, jj] = (vi[:, sl] * c - vr[:, sl] * s).astype(BF16)


def _dft_real_kernel(m_ref, v_ref, o_ref, *, kdim):
    mat = m_ref[...]
    o_ref[0] = (_dot(mat[:, :kdim], v_ref[0, 0]) + _dot(mat[:, kdim:], v_ref[0, 1])).astype(BF16)


def _dft_real_call(mat, v, tc):
    b, _, kdim, c = v.shape
    mrows = mat.shape[0]
    return pl.pallas_call(
        functools.partial(_dft_real_kernel, kdim=kdim),
        grid=(b, c // tc),
        in_specs=[pl.BlockSpec(mat.shape, lambda bi, j: (0, 0)),
                  pl.BlockSpec((1, 2, kdim, tc), lambda bi, j: (bi, 0, 0, j))],
        out_specs=pl.BlockSpec((1, mrows, tc), lambda bi, j: (bi, 0, j)),
        out_shape=jax.ShapeDtypeStruct((b, mrows, c), BF16),
        compiler_params=_cparams(2),
        name="dft_real",
    )(mat, v)


def _cos_sin(n):
    idx = np.arange(n, dtype=np.int64)
    ang = 2.0 * np.pi * ((idx[:, None] * idx[None, :]) % n) / n
    return np.cos(ang), np.sin(ang)


def _fourier_positions(u):
    b, _, n, c = u.shape
    scale = 1.0 / math.sqrt(n * FOURIER_GROUP_DIM)
    if n <= 256:
        cm, sm = _cos_sin(n)
        mat = jnp.asarray(np.concatenate([cm, sm], axis=1) * scale, F32).astype(BF16)
        return _dft_real_call(mat, u, c)
    n1, n2 = DFT_N1, n // DFT_N1
    c1, s1 = _cos_sin(n1)
    f1 = jnp.asarray(np.block([[c1, s1], [-s1, c1]]), F32).astype(BF16)
    k1 = np.arange(n1, dtype=np.int64)[:, None]
    t2 = np.arange(n2, dtype=np.int64)[None, :]
    ang = 2.0 * np.pi * ((k1 * t2) % n) / n
    tt2 = 4
    tw_shape = lambda a: jnp.asarray(a.reshape(n1, n2 // tt2, tt2).transpose(1, 0, 2), F32)
    twc, tws = tw_shape(np.cos(ang)), tw_shape(np.sin(ang))
    v = pl.pallas_call(
        functools.partial(_dft_stage1_kernel, tt2=tt2),
        grid=(n2 // tt2, b),
        in_specs=[pl.BlockSpec((1, 2, n1, tt2 * c), lambda j, bi: (bi, 0, 0, j)),
                  pl.BlockSpec((2 * n1, 2 * n1), lambda j, bi: (0, 0)),
                  pl.BlockSpec((1, n1, tt2), lambda j, bi: (j, 0, 0)),
                  pl.BlockSpec((1, n1, tt2), lambda j, bi: (j, 0, 0))],
        out_specs=pl.BlockSpec((1, 2, tt2, n1, c), lambda j, bi: (bi, 0, j, 0, 0)),
        out_shape=jax.ShapeDtypeStruct((b, 2, n2, n1, c), BF16),
        compiler_params=_cparams(2),
        name="dft_stage1",
    )(u.reshape(b, 2, n1, n2 * c), f1, twc, tws)
    c2, s2 = _cos_sin(n2)
    mat = jnp.asarray(np.concatenate([c2, s2], axis=1) * scale, F32).astype(BF16)
    y = _dft_real_call(mat, v.reshape(b, 2, n2, n1 * c), 8192)
    return y.reshape(b, n, c)


def _post_kernel(*refs, pieces):
    x_ref, mod_ref = refs[:2]
    mix, pos = [], 2
    for n_pieces in pieces:
        mix.append((refs[pos:pos + n_pieces], refs[pos + n_pieces]))
        pos += n_pieces + 1
    g1_ref, b1_ref, wg_ref, wu_ref, wd_ref, g2_ref, b2_ref, o_ref = refs[pos:]
    m = mod_ref[0]
    sub = _row_subtiles(x_ref.shape[1])
    x1s, hs, fs = [], [], []
    for r, rows in enumerate(sub):
        y = None
        for parts, w_ref in mix:
            a = parts[0][0, rows] if len(parts) == 1 else parts[r][0]
            t = _dot(a, w_ref[...])
            y = t if y is None else y + t
        x1 = _layer_norm(DN_ALPHA * x_ref[0, rows] + m[2:3] * y) * g1_ref[...] + b1_ref[...]
        x1s.append(x1)
        hs.append((_layer_norm(x1) * (1.0 + m[4:5]) + m[3:4]).astype(BF16))
    for h in hs:
        f = None
        lo = 0
        for width in FF_CHUNKS:
            sl = slice(lo, lo + width)
            lo += width
            act = _silu(_dot(h, wg_ref[:, sl])) * _dot(h, wu_ref[:, sl])
            t = _dot(act.astype(BF16), wd_ref[sl, :])
            f = t if f is None else f + t
        fs.append(f)
    for rows, x1, f in zip(sub, x1s, fs):
        o_ref[0, rows] = _layer_norm(DN_ALPHA * x1 + m[5:6] * f) * g2_ref[...] + b2_ref[...]


def _post_call(x, mod, mod_row, mixes, g1, b1, wg, wu, wd, g2, b2, tm):
    b, n, _ = x.shape
    const = lambda a: pl.BlockSpec(a.shape, lambda bi, i: (0,) * a.ndim, pipeline_mode=pl.Buffered(1))
    in_specs = [pl.BlockSpec((1, tm, D), lambda bi, i: (bi, i, 0)),
                pl.BlockSpec((1, 6, D), lambda bi, i: (mod_row(bi), 0, 0))]
    args = [x, mod]
    for parts, w in mixes:
        assert len(parts) in (1, len(_row_subtiles(tm)))
        for a in parts:
            in_specs.append(pl.BlockSpec((1, tm // len(parts), a.shape[2]), lambda bi, i: (bi, i, 0)))
        in_specs.append(const(w))
        args += [*parts, w]
    tail = [g1, b1, wg, wu, wd, g2, b2]
    in_specs += [const(a) for a in tail]
    return pl.pallas_call(
        functools.partial(_post_kernel, pieces=tuple(len(parts) for parts, _ in mixes)),
        grid=(b, n // tm),
        in_specs=in_specs,
        out_specs=pl.BlockSpec((1, tm, D), lambda bi, i: (bi, i, 0)),
        out_shape=jax.ShapeDtypeStruct((b, n, D), F32),
        compiler_params=_cparams(2),
        name="post",
    )(*args, *tail)


def _proj1_kernel(x_ref, mod_ref, rtab_ref, ctab_ref, mask_ref, w_ref, q_ref, k_ref, v_ref):
    m = mod_ref[0]
    sub = _row_subtiles(x_ref.shape[1])
    hs = [(_layer_norm(x_ref[0, rows]) * (1.0 + m[1:2]) + m[0:1]).astype(BF16) for rows in sub]
    us = [_dot(h, w_ref[...]) for h in hs]
    width = DIFF_HEADS * LANES
    for rows, u in zip(sub, us):
        cos, sa, sb = _rope_tile(rtab_ref, ctab_ref, mask_ref, rows)
        lane = lax.broadcasted_iota(jnp.int32, (u.shape[0], LANES), 1)
        first = lane < DIFF_HEAD_DIM
        ones_lane = (lane == 0).astype(BF16)
        for hd in range(DIFF_HEADS):
            sl = slice(hd * LANES, (hd + 1) * LANES)
            q = _rope(u[:, sl], cos, sa, sb, DIFF_HEAD_DIM // 4) * (DIFF_SCALE * LOG2E)
            q_ref[0, hd, 0, rows] = jnp.where(first, q, 0.0).astype(BF16)
            q_ref[0, hd, 1, rows] = jnp.where(first, 0.0, q).astype(BF16)
            ksl = slice(width + hd * LANES, width + (hd + 1) * LANES)
            k_ref[0, hd, rows] = _rope(u[:, ksl], cos, sa, sb, DIFF_HEAD_DIM // 4).astype(BF16)
            vsl = slice(2 * width + hd * LANES, 2 * width + (hd + 1) * LANES)
            v_ref[0, hd, rows, :LANES] = u[:, vsl].astype(BF16)
            v_ref[0, hd, rows, LANES:] = ones_lane


def _proj1_call(x, mod, mod_row, tables, w, tm):
    b, n, _ = x.shape
    head = pl.BlockSpec((1, DIFF_HEADS, tm, LANES), lambda bi, i: (bi, 0, i, 0))
    return pl.pallas_call(
        _proj1_kernel,
        grid=(b, n // tm),
        in_specs=[pl.BlockSpec((1, tm, D), lambda bi, i: (bi, i, 0)),
                  pl.BlockSpec((1, 6, D), lambda bi, i: (mod_row(bi), 0, 0)),
                  *_rope_specs(tm),
                  pl.BlockSpec(w.shape, lambda bi, i: (0, 0))],
        out_specs=[pl.BlockSpec((1, DIFF_HEADS, 2, tm, LANES), lambda bi, i: (bi, 0, 0, i, 0)), head,
                   pl.BlockSpec((1, DIFF_HEADS, tm, 2 * LANES), lambda bi, i: (bi, 0, i, 0))],
        out_shape=[jax.ShapeDtypeStruct((b, DIFF_HEADS, 2, n, LANES), BF16),
                   jax.ShapeDtypeStruct((b, DIFF_HEADS, n, LANES), BF16),
                   jax.ShapeDtypeStruct((b, DIFF_HEADS, n, 2 * LANES), BF16)],
        compiler_params=_cparams(2),
        name="proj1",
    )(x, mod, *tables, w)


def _pad_heads(w, heads, width):
    kdim = w.shape[0]
    w3 = jnp.pad(w.reshape(kdim, heads, width), ((0, 0), (0, 0), (0, LANES - width)))
    return w3.reshape(kdim, heads * LANES)


def _layer0_weights(w_in, q_norm, w_uq, kv_norm, w_ukv):
    o3 = FOURIER_WIDTH + MLA_Q_LORA + MLA_KV_LORA
    w_kr = jnp.zeros((D, LANES), F32).at[:, MLA_NOPE:MLA_NOPE + MLA_ROPE].set(w_in[:, o3:])
    wa = jnp.concatenate([w_in[:, :o3], w_kr], axis=1).astype(BF16)
    cc, sc = _cos_sin(FOURIER_GROUP_DIM)
    dft = jnp.asarray(np.concatenate([cc, -sc], axis=1), F32).astype(BF16)
    wuq = _pad_heads(w_uq, MLA_HEADS, MLA_NOPE + MLA_ROPE).astype(BF16)
    ukv = w_ukv.reshape(MLA_KV_LORA, MLA_HEADS, MLA_NOPE + MLA_V)
    wk = _pad_heads(ukv[:, :, :MLA_NOPE].reshape(MLA_KV_LORA, -1), MLA_HEADS, MLA_NOPE).astype(BF16)
    wv = _pad_heads(ukv[:, :, MLA_NOPE:].reshape(MLA_KV_LORA, -1), MLA_HEADS, MLA_V).astype(BF16)
    return wa, dft, q_norm.reshape(1, -1), wuq, kv_norm.reshape(1, -1), wk, wv


def kernel(x, c, ctx, c_ctx,
           l0_w_mod, l0_b_mod, l0_w_in, l0_q_norm, l0_w_uq, l0_kv_norm, l0_w_ukv, l0_w_out,
           l0_ln1_g, l0_ln1_b, l0_w_gate, l0_w_up, l0_w_down, l0_ln2_g, l0_ln2_b,
           l1_w_mod, l1_b_mod, l1_w_in, l1_lambda_q1, l1_lambda_k1, l1_lambda_q2, l1_lambda_k2,
           l1_subln, l1_w_out, l1_ln1_g, l1_ln1_b, l1_w_gate, l1_w_up, l1_w_down, l1_ln2_g, l1_ln2_b):
    b, n, _ = x.shape
    nc = ctx.shape[1]
    tm, tq = 512, 256
    assert b + 1 <= MOD_ROWS and n % max(tm, ATTN_TILES * tq) == 0 and nc % LANES == 0
    ctx_row = b
    lat_row = lambda bi: bi
    cx_row = lambda bi: ctx_row
    row = lambda a: a.reshape(1, -1)

    c_all = jnp.zeros((MOD_ROWS, D), F32).at[:b].set(c).at[b].set(c_ctx)
    mod0 = _mod_call(c_all, l0_w_mod, l0_b_mod)
    mod1 = _mod_call(c_all, l1_w_mod, l1_b_mod)
    ident = _identity_tables(nc)

    w0 = _layer0_weights(l0_w_in, l0_q_norm, l0_w_uq, l0_kv_norm, l0_w_ukv)
    tab0 = _rope_tables(n, MLA_ROPE, [MLA_NOPE])
    u, q, k, v = _proj0_call(x, mod0, lat_row, tab0, w0, tm)
    uc, qc, kc, vc = _proj0_call(ctx, mod0, cx_row, ident, w0, nc)
    att = _mla_attn_call(q, kc, vc, k, v, tq)
    attc = _mla_ctx_attn_call(qc, kc, vc)
    four = _fourier_positions(u)
    fourc = _fourier_positions(uc)
    wo = l0_w_out.astype(BF16)
    ffn0 = (row(l0_ln1_g), row(l0_ln1_b), l0_w_gate.astype(BF16), l0_w_up.astype(BF16), l0_w_down.astype(BF16),
            row(l0_ln2_g), row(l0_ln2_b))
    x = _post_call(x, mod0, lat_row, [([four], wo[:FOURIER_WIDTH]), ([att], wo[FOURIER_WIDTH:])], *ffn0, tm)
    xc = _post_call(ctx, mod0, cx_row, [([fourc], wo[:FOURIER_WIDTH]), ([attc], wo[FOURIER_WIDTH:])], *ffn0, nc)

    lambda_init = 0.8 - 0.6 * math.exp(-0.3 * 1)
    w1 = l1_w_in.astype(BF16)
    tab1 = _rope_tables(n, DIFF_HEAD_DIM, [0, DIFF_HEAD_DIM])
    q, k, v = _proj1_call(x, mod1, lat_row, tab1, w1, tm)
    _, kc, vc = _proj1_call(xc, mod1, cx_row, ident, w1, nc)
    lam_params = jnp.stack([l1_lambda_q1, l1_lambda_k1, l1_lambda_q2, l1_lambda_k2])
    att = _diff_attn_call(lam_params, row(l1_subln), q, kc, vc, k, v, tq, lambda_init)
    ffn1 = (row(l1_ln1_g), row(l1_ln1_b), l1_w_gate.astype(BF16), l1_w_up.astype(BF16), l1_w_down.astype(BF16),
            row(l1_ln2_g), row(l1_ln2_b))
    return _post_call(x, mod1, lat_row, [([att], l1_w_out.astype(BF16))], *ffn1, tm)
```

```python
import functools
import math

import numpy as np
import jax
import jax.numpy as jnp
from jax import lax
from jax.experimental import pallas as pl
from jax.experimental.pallas import tpu as pltpu

F32 = jnp.float32
BF16 = jnp.bfloat16

D = 1024
DEPTH = 2
GRID_W = 64
ROPE_BASE = 10000.0
LN_EPS = 1e-6
RMS_EPS = 1e-6
DN_ALPHA = (2 * DEPTH) ** 0.25
LANES = 128
FOURIER_GROUPS = 4
FOURIER_GROUP_DIM = 128
FOURIER_WIDTH = 512
MLA_HEADS = 8
MLA_Q_LORA = 256
MLA_KV_LORA = 256
MLA_NOPE = 64
MLA_ROPE = 32
MLA_V = 64
MLA_SCALE = (MLA_NOPE + MLA_ROPE) ** -0.5
DIFF_HEADS = 8
DIFF_HEAD_DIM = 64
DIFF_SCALE = DIFF_HEAD_DIM ** -0.5
LOG2E = math.log2(math.e)
MXU_TILE = 256
FF_HIDDEN = 2816
FF_CHUNKS = (1536, 1280)
ROW_SUBTILES = 2
MOD_ROWS = 8
DFT_N1 = 128
ATTN_TILES = 4
VMEM_LIMIT = 56 * 2 ** 20


def _cparams(n_axes):
    return pltpu.CompilerParams(dimension_semantics=("arbitrary",) * n_axes, vmem_limit_bytes=VMEM_LIMIT)


def _layer_norm(x):
    mu = jnp.mean(x, axis=-1, keepdims=True)
    xc = x - mu
    var = jnp.mean(xc * xc, axis=-1, keepdims=True)
    return xc * lax.rsqrt(var + LN_EPS)


def _rms(x):
    return x * lax.rsqrt(jnp.mean(x * x, axis=-1, keepdims=True) + RMS_EPS)


def _silu(x):
    return x * (1.0 / (1.0 + jnp.exp(-x)))


def _rope(x, cos, sa, sb, q):
    return x * cos + pltpu.roll(x, LANES - q, 1) * sa + pltpu.roll(x, q, 1) * sb


def _row_subtiles(tm):
    n_sub = ROW_SUBTILES if tm % (ROW_SUBTILES * MXU_TILE) == 0 else 1
    return [slice(r * (tm // n_sub), (r + 1) * (tm // n_sub)) for r in range(n_sub)]


def _dot(a, b):
    return jnp.dot(a, b, preferred_element_type=F32)


def _dot_nt(a, b):
    return lax.dot_general(a, b, (((1,), (1,)), ((), ())), preferred_element_type=F32)


def _mod_kernel(c_ref, w_ref, b_ref, o_ref):
    o_ref[...] = _dot(_silu(c_ref[...]), w_ref[...]) + b_ref[...]


def _mod_call(c_all, w, b):
    n_out = w.shape[1]
    tn = 1024
    return pl.pallas_call(
        _mod_kernel,
        grid=(n_out // tn,),
        in_specs=[pl.BlockSpec((MOD_ROWS, D), lambda j: (0, 0)),
                  pl.BlockSpec((D, tn), lambda j: (0, j)),
                  pl.BlockSpec((1, tn), lambda j: (0, j))],
        out_specs=pl.BlockSpec((MOD_ROWS, tn), lambda j: (0, j)),
        out_shape=jax.ShapeDtypeStruct((MOD_ROWS, n_out), F32),
        compiler_params=_cparams(1),
        name="adaln_mod",
    )(c_all, w, b.reshape(1, n_out)).reshape(MOD_ROWS, 6, D)


def _rope_tables(n, rope_dim, regions):
    half, quarter = rope_dim // 2, rope_dim // 4
    is_rope = np.zeros(LANES, bool)
    use_col = np.zeros(LANES, bool)
    is_x1 = np.zeros(LANES, bool)
    fi = np.zeros(LANES, np.int32)
    for off in regions:
        for r in range(rope_dim):
            j = off + r
            rr = r % half
            is_rope[j] = True
            use_col[j] = r >= half
            is_x1[j] = rr < quarter
            fi[j] = rr % quarter
    inv = (1.0 / (ROPE_BASE ** (jnp.arange(quarter, dtype=F32) / quarter)))[fi][None, :]
    ang_r = jnp.arange(n // GRID_W, dtype=jnp.int32).astype(F32)[:, None] * inv
    ang_c = jnp.arange(GRID_W, dtype=jnp.int32).astype(F32)[:, None] * inv

    def tables(ang):
        sin = jnp.sin(ang)
        return jnp.stack([jnp.where(is_rope[None, :], jnp.cos(ang), 1.0),
                          jnp.where((is_rope & is_x1)[None, :], -sin, 0.0),
                          jnp.where((is_rope & ~is_x1)[None, :], sin, 0.0)])

    return tables(ang_r), tables(ang_c), jnp.asarray(use_col[None, :], F32)


def _identity_tables(n):
    one = lambda rows: jnp.stack([jnp.ones((rows, LANES), F32), jnp.zeros((rows, LANES), F32),
                                  jnp.zeros((rows, LANES), F32)])
    return one(n // GRID_W), one(GRID_W), jnp.zeros((1, LANES), F32)


def _rope_specs(tm):
    return [pl.BlockSpec((3, tm // GRID_W, LANES), lambda bi, i: (0, i, 0)),
            pl.BlockSpec((3, GRID_W, LANES), lambda bi, i: (0, 0, 0)),
            pl.BlockSpec((1, LANES), lambda bi, i: (0, 0))]


def _rope_tile(rtab_ref, ctab_ref, mask_ref, rows):
    g0, g1 = rows.start // GRID_W, rows.stop // GRID_W
    by_col = mask_ref[...] > 0.5
    out = []
    for t in range(3):
        r = jnp.broadcast_to(rtab_ref[t, g0:g1][:, None, :], (g1 - g0, GRID_W, LANES))
        c = jnp.concatenate([ctab_ref[t]] * (g1 - g0), axis=0)
        out.append(jnp.where(by_col, c, r.reshape(rows.stop - rows.start, LANES)))
    return out


def _proj0_kernel(x_ref, mod_ref, rtab_ref, ctab_ref, mask_ref, wa_ref, dft_ref, qn_ref, wuq_ref, kvn_ref,
                  wk_ref, wv_ref, u_ref, q_ref, k_ref, v_ref, *scratch, n2):
    m = mod_ref[0]
    sub = _row_subtiles(x_ref.shape[1])
    hs = [(_layer_norm(x_ref[0, rows]) * (1.0 + m[1:2]) + m[0:1]).astype(BF16) for rows in sub]
    us = [_dot(h, wa_ref[...]) for h in hs]
    o1 = FOURIER_WIDTH
    o2 = o1 + MLA_Q_LORA
    o3 = o2 + MLA_KV_LORA
    for r, (rows, u) in enumerate(zip(sub, us)):
        for g in range(FOURIER_GROUPS):
            lo = g * FOURIER_GROUP_DIM
            z = _dot(u[:, lo:lo + FOURIER_GROUP_DIM].astype(BF16), dft_ref[...])
            for part in range(2):
                zp = z[:, part * FOURIER_GROUP_DIM:(part + 1) * FOURIER_GROUP_DIM]
                if n2 is None:
                    u_ref[0, part, rows, lo:lo + FOURIER_GROUP_DIM] = zp.astype(BF16)
                else:
                    scratch[0][r, part, g] = zp
        if n2 is not None:
            cnt = (rows.stop - rows.start) // n2
            g0 = rows.start // n2
            for t2 in range(n2):
                for part in range(2):
                    for g in range(FOURIER_GROUPS):
                        lo = t2 * FOURIER_WIDTH + g * FOURIER_GROUP_DIM
                        u_ref[0, part, g0:g0 + cnt, lo:lo + FOURIER_GROUP_DIM] = (
                            scratch[0][r, part, g, pl.ds(t2, cnt, stride=n2), :])
    cqs = [(_rms(u[:, o1:o2]) * qn_ref[...]).astype(BF16) for u in us]
    ckvs = [(_rms(u[:, o2:o3]) * kvn_ref[...]).astype(BF16) for u in us]
    qs = [_dot(cq, wuq_ref[...]) for cq in cqs]
    kns = [_dot(ckv, wk_ref[...]) for ckv in ckvs]
    vs = [_dot(ckv, wv_ref[...]) for ckv in ckvs]
    for rows, u, q, kn, v in zip(sub, us, qs, kns, vs):
        cos, sa, sb = _rope_tile(rtab_ref, ctab_ref, mask_ref, rows)
        kr = _rope(u[:, o3:o3 + LANES], cos, sa, sb, MLA_ROPE // 4)
        ones_lane = (lax.broadcasted_iota(jnp.int32, (u.shape[0], LANES), 1) == MLA_V).astype(F32)
        for hd in range(MLA_HEADS):
            sl = slice(hd * LANES, (hd + 1) * LANES)
            q_ref[0, hd, rows] = (_rope(q[:, sl], cos, sa, sb, MLA_ROPE // 4) * (MLA_SCALE * LOG2E)).astype(BF16)
            k_ref[0, hd, rows] = (kn[:, sl] + kr).astype(BF16)
            v_ref[0, hd, rows] = (v[:, sl] + ones_lane).astype(BF16)


def _proj0_call(x, mod, mod_row, tables, w, tm, n2=None):
    b, n, _ = x.shape
    wa, dft, qn, wuq, kvn, wk, wv = w
    const = lambda shape: pl.BlockSpec(shape, lambda bi, i: (0,) * len(shape))
    head = pl.BlockSpec((1, MLA_HEADS, tm, LANES), lambda bi, i: (bi, 0, i, 0))
    if n2 is None:
        u_spec = pl.BlockSpec((1, 2, tm, FOURIER_WIDTH), lambda bi, i: (bi, 0, i, 0))
        u_shape = jax.ShapeDtypeStruct((b, 2, n, FOURIER_WIDTH), BF16)
        scratch = []
    else:
        u_spec = pl.BlockSpec((1, 2, tm // n2, n2 * FOURIER_WIDTH), lambda bi, i: (bi, 0, i, 0))
        u_shape = jax.ShapeDtypeStruct((b, 2, n // n2, n2 * FOURIER_WIDTH), F32)
        n_sub = len(_row_subtiles(tm))
        scratch = [pltpu.VMEM((n_sub, 2, FOURIER_GROUPS, tm // n_sub, FOURIER_GROUP_DIM), F32)]
    return pl.pallas_call(
        functools.partial(_proj0_kernel, n2=n2),
        grid=(b, n // tm),
        in_specs=[pl.BlockSpec((1, tm, D), lambda bi, i: (bi, i, 0)),
                  pl.BlockSpec((1, 6, D), lambda bi, i: (mod_row(bi), 0, 0)),
                  *_rope_specs(tm),
                  const(wa.shape), const(dft.shape), const(qn.shape), const(wuq.shape), const(kvn.shape),
                  const(wk.shape), const(wv.shape)],
        out_specs=[u_spec, head, head, head],
        out_shape=[u_shape] + [jax.ShapeDtypeStruct((b, MLA_HEADS, n, LANES), BF16)] * 3,
        scratch_shapes=scratch,
        compiler_params=_cparams(2),
        name="proj0",
    )(x, mod, *tables, wa, dft, qn, wuq, kvn, wk, wv)


def _softmax_pv(q, kc, vc, k, v):
    s_c = _dot_nt(q, kc)
    m = jnp.max(s_c, axis=-1, keepdims=True)
    if k is not None:
        s_l = _dot_nt(q, k)
        m = jnp.maximum(m, jnp.max(s_l, axis=-1, keepdims=True))
    acc = _dot(jnp.exp2(s_c - m).astype(BF16), vc)
    if k is not None:
        acc = acc + _dot(jnp.exp2(s_l - m).astype(BF16), v)
    return acc


def _merge_head_pair(o_even, o_odd):
    lane = lax.broadcasted_iota(jnp.int32, o_even.shape, 1)
    return jnp.where(lane < MLA_V, o_even, pltpu.roll(o_odd, MLA_V, 1))


def _mla_ctx_attn_kernel(q_ref, kc_ref, vc_ref, o_ref):
    outs = []
    for hh in range(2):
        acc = _softmax_pv(q_ref[0, hh], kc_ref[0, hh], vc_ref[0, hh], None, None)
        outs.append(acc / acc[:, MLA_V:MLA_V + 1])
    o_ref[0] = _merge_head_pair(*outs).astype(BF16)


def _mla_ctx_attn_call(q, kc, vc):
    b, h, nc, _ = q.shape
    pair = pl.BlockSpec((1, 2, nc, LANES), lambda bi, hp: (bi, hp, 0, 0))
    return pl.pallas_call(
        _mla_ctx_attn_kernel,
        grid=(b, h // 2),
        in_specs=[pair, pair, pair],
        out_specs=pl.BlockSpec((1, nc, LANES), lambda bi, hp: (bi, 0, hp)),
        out_shape=jax.ShapeDtypeStruct((b, nc, h // 2 * LANES), BF16),
        compiler_params=_cparams(2),
        name="mla_ctx_attn",
    )(q, kc, vc)


def _scores_into(q, kc, k, s_ref, m_ref, slot):
    nc = kc.shape[0]
    s_c = _dot_nt(q, kc)
    s_l = _dot_nt(q, k)
    s_ref[slot, :, :nc] = s_c
    s_ref[slot, :, nc:] = s_l
    m_ref[slot] = jnp.maximum(jnp.max(s_c, axis=-1, keepdims=True), jnp.max(s_l, axis=-1, keepdims=True))


def _values_from(s_ref, m_ref, slot, vc, v):
    nc = vc.shape[0]
    m = m_ref[slot]
    return (_dot(jnp.exp2(s_ref[slot, :, :nc] - m).astype(BF16), vc)
            + _dot(jnp.exp2(s_ref[slot, :, nc:] - m).astype(BF16), v))


def _init_pipeline(scratch_refs):
    @pl.when(pl.program_id(0) == 0)
    def _():
        for r in scratch_refs:
            r[...] = jnp.zeros(r.shape, r.dtype)


def _mla_attn_kernel(q_ref, kc_ref, k_ref, vcp_ref, vp_ref, vce_ref, ve_ref, vco_ref, vo_ref, o_ref,
                     s_ref, m_ref, half_ref, stash_ref):
    _init_pipeline((s_ref, m_ref, half_ref, stash_ref))
    tq = half_ref.shape[0]
    n_tiles = o_ref.shape[1] // tq
    tile = lambda i: slice(i * tq, (i + 1) * tq)

    def head_out(slot, vc_ref, v_ref):
        acc = _values_from(s_ref, m_ref, slot, vc_ref[0, 0], v_ref[0, 0])
        return acc / acc[:, MLA_V:MLA_V + 1]

    _scores_into(q_ref[0, 0, tile(0)], kc_ref[0, 0], k_ref[0, 0], s_ref, m_ref, 0)
    o_ref[0, tile(n_tiles - 1)] = _merge_head_pair(half_ref[...], head_out(1, vcp_ref, vp_ref)).astype(BF16)
    o_ref[0, :(n_tiles - 1) * tq] = stash_ref[...]
    for i in range(n_tiles):
        _scores_into(q_ref[0, 1, tile(i)], kc_ref[0, 1], k_ref[0, 1], s_ref, m_ref, 1)
        even = head_out(0, vce_ref, ve_ref)
        if i + 1 < n_tiles:
            _scores_into(q_ref[0, 0, tile(i + 1)], kc_ref[0, 0], k_ref[0, 0], s_ref, m_ref, 0)
            stash_ref[tile(i)] = _merge_head_pair(even, head_out(1, vco_ref, vo_ref)).astype(BF16)
        else:
            half_ref[...] = even


def _tile_maps(n_b, n_h, n_q):
    last = n_b * n_h * n_q - 1

    def unravel(t):
        return t // (n_h * n_q), (t // n_q) % n_h, t % n_q

    cur = lambda t: unravel(jnp.minimum(t, last))
    prev = lambda t: unravel(jnp.maximum(t - 1, 0))
    return last + 2, cur, prev


def _mla_attn_call(q, kc, vc, k, v, tq):
    b, h, n, _ = q.shape
    nc = kc.shape[2]
    rows = ATTN_TILES * tq
    steps, cur, prev = _tile_maps(b, h // 2, n // rows)

    def pair(size, blk):
        def index(t):
            bi, hp, j = cur(t)
            return (bi, hp, j if blk else 0, 0)
        return pl.BlockSpec((1, 2, size, LANES), index)

    def one(size, which, parity):
        def index(t):
            bi, hp, _ = which(t)
            return (bi, 2 * hp + parity, 0, 0)
        return pl.BlockSpec((1, 1, size, LANES), index)

    def out_index(t):
        bi, hp, j = prev(t)
        return (bi, j, hp)

    return pl.pallas_call(
        _mla_attn_kernel,
        grid=(steps,),
        in_specs=[pair(rows, True), pair(nc, False), pair(n, False),
                  one(nc, prev, 1), one(n, prev, 1), one(nc, cur, 0), one(n, cur, 0), one(nc, cur, 1), one(n, cur, 1)],
        out_specs=pl.BlockSpec((1, rows, LANES), out_index),
        out_shape=jax.ShapeDtypeStruct((b, n, h // 2 * LANES), BF16),
        scratch_shapes=[pltpu.VMEM((2, tq, nc + n), F32), pltpu.VMEM((2, tq, 1), F32),
                        pltpu.VMEM((tq, LANES), F32), pltpu.VMEM((rows - tq, LANES), BF16)],
        compiler_params=_cparams(1),
        name="mla_attn",
    )(q, kc, k, vc, v, vc, v, vc, v)


def _diff_attn_kernel(lam_ref, sub_ref, q_ref, kc_ref, k_ref, vcp_ref, vp_ref, vcc_ref, vcur_ref, o_ref,
                      s_ref, m_ref, o0_ref, stash_ref, *, lambda_init):
    _init_pipeline((s_ref, m_ref, o0_ref, stash_ref))
    lp = lam_ref[...]
    lam = (jnp.exp(jnp.sum(lp[0:1] * lp[1:2], axis=-1, keepdims=True))
           - jnp.exp(jnp.sum(lp[2:3] * lp[3:4], axis=-1, keepdims=True)) + lambda_init)
    kc, k = kc_ref[0, 0], k_ref[0, 0]
    tq = o0_ref.shape[0]
    n_tiles = o_ref.shape[1] // tq
    tile = lambda i: slice(i * tq, (i + 1) * tq)

    def map_out(slot, vc_ref, v_ref):
        acc = _values_from(s_ref, m_ref, slot, vc_ref[0, 0], v_ref[0, 0])
        return acc[:, :LANES] / acc[:, LANES:LANES + 1]

    def finish(o0, o1):
        return (_rms(o0 - lam * o1) * sub_ref[...] * (1.0 - lambda_init)).astype(BF16)

    _scores_into(q_ref[0, 0, 0, tile(0)], kc, k, s_ref, m_ref, 0)
    o_ref[0, tile(n_tiles - 1)] = finish(o0_ref[...], map_out(1, vcp_ref, vp_ref))
    o_ref[0, :(n_tiles - 1) * tq] = stash_ref[...]
    for i in range(n_tiles):
        _scores_into(q_ref[0, 0, 1, tile(i)], kc, k, s_ref, m_ref, 1)
        o0 = map_out(0, vcc_ref, vcur_ref)
        if i + 1 < n_tiles:
            _scores_into(q_ref[0, 0, 0, tile(i + 1)], kc, k, s_ref, m_ref, 0)
            stash_ref[tile(i)] = finish(o0, map_out(1, vcc_ref, vcur_ref))
        else:
            o0_ref[...] = o0


def _diff_attn_call(lam_params, subln, q, kc, vc, k, v, tq, lambda_init):
    b, h, _, n, _ = q.shape
    nc = kc.shape[2]
    rows = ATTN_TILES * tq
    steps, cur, prev = _tile_maps(b, h, n // rows)

    def head(size, width, which):
        def index(t):
            bi, hd, _ = which(t)
            return (bi, hd, 0, 0)
        return pl.BlockSpec((1, 1, size, width), index)

    def q_index(t):
        bi, hd, j = cur(t)
        return (bi, hd, 0, j, 0)

    def out_index(t):
        bi, hd, j = prev(t)
        return (bi, j, hd)

    return pl.pallas_call(
        functools.partial(_diff_attn_kernel, lambda_init=lambda_init),
        grid=(steps,),
        in_specs=[pl.BlockSpec((4, DIFF_HEAD_DIM), lambda t: (0, 0)),
                  pl.BlockSpec((1, LANES), lambda t: (0, 0)),
                  pl.BlockSpec((1, 1, 2, rows, LANES), q_index),
                  head(nc, LANES, cur), head(n, LANES, cur),
                  head(nc, 2 * LANES, prev), head(n, 2 * LANES, prev),
                  head(nc, 2 * LANES, cur), head(n, 2 * LANES, cur)],
        out_specs=pl.BlockSpec((1, rows, LANES), out_index),
        out_shape=jax.ShapeDtypeStruct((b, n, h * LANES), BF16),
        scratch_shapes=[pltpu.VMEM((2, tq, nc + n), F32), pltpu.VMEM((2, tq, 1), F32),
                        pltpu.VMEM((tq, LANES), F32), pltpu.VMEM((rows - tq, LANES), BF16)],
        compiler_params=_cparams(1),
        name="diff_attn",
    )(lam_params, subln, q, kc, k, vc, v, vc, v)


def _dft_stage1_kernel(u_ref, f_ref, twc_ref, tws_ref, v_ref, *, tt2):
    f = f_ref[...]
    r = _dot(f[:, :DFT_N1], u_ref[0, 0].astype(BF16)) + _dot(f[:, DFT_N1:], u_ref[0, 1].astype(BF16))
    vr, vi = r[:DFT_N1], r[DFT_N1:]
    twc, tws = twc_ref[0], tws_ref[0]
    for jj in range(tt2):
        c, s = twc[:, jj:jj + 1], tws[:, jj:jj + 1]
        sl = slice(jj * FOURIER_WIDTH, (jj + 1) * FOURIER_WIDTH)
        v_ref[0, 0, jj] = (vr[:, sl] * c + vi[:, sl] * s).astype(BF16)
        v_ref[0, 1, jj] = (vi[:, sl] * c - vr[:, sl] * s).astype(BF16)


def _dft_real_kernel(m_ref, v_ref, o_ref, *, kdim):
    mat = m_ref[...]
    o_ref[0] = (_dot(mat[:, :kdim], v_ref[0, 0]) + _dot(mat[:, kdim:], v_ref[0, 1])).astype(BF16)


def _dft_real_call(mat, v, tc):
    b, _, kdim, c = v.shape
    mrows = mat.shape[0]
    return pl.pallas_call(
        functools.partial(_dft_real_kernel, kdim=kdim),
        grid=(b, c // tc),
        in_specs=[pl.BlockSpec(mat.shape, lambda bi, j: (0, 0)),
                  pl.BlockSpec((1, 2, kdim, tc), lambda bi, j: (bi, 0, 0, j))],
        out_specs=pl.BlockSpec((1, mrows, tc), lambda bi, j: (bi, 0, j)),
        out_shape=jax.ShapeDtypeStruct((b, mrows, c), BF16),
        compiler_params=_cparams(2),
        name="dft_real",
    )(mat, v)


def _cos_sin(n):
    idx = np.arange(n, dtype=np.int64)
    ang = 2.0 * np.pi * ((idx[:, None] * idx[None, :]) % n) / n
    return np.cos(ang), np.sin(ang)


def _fourier_positions_direct(u):
    n, c = u.shape[2], u.shape[3]
    cm, sm = _cos_sin(n)
    scale = 1.0 / math.sqrt(n * FOURIER_GROUP_DIM)
    mat = jnp.asarray(np.concatenate([cm, sm], axis=1) * scale, F32).astype(BF16)
    return _dft_real_call(mat, u, c)


def _fourier_positions(u, n):
    b, c = u.shape[0], FOURIER_WIDTH
    scale = 1.0 / math.sqrt(n * FOURIER_GROUP_DIM)
    n1, n2 = DFT_N1, n // DFT_N1
    c1, s1 = _cos_sin(n1)
    f1 = jnp.asarray(np.block([[c1, s1], [-s1, c1]]), F32).astype(BF16)
    k1 = np.arange(n1, dtype=np.int64)[:, None]
    t2 = np.arange(n2, dtype=np.int64)[None, :]
    ang = 2.0 * np.pi * ((k1 * t2) % n) / n
    tt2 = 4
    tw_shape = lambda a: jnp.asarray(a.reshape(n1, n2 // tt2, tt2).transpose(1, 0, 2), F32)
    twc, tws = tw_shape(np.cos(ang)), tw_shape(np.sin(ang))
    v = pl.pallas_call(
        functools.partial(_dft_stage1_kernel, tt2=tt2),
        grid=(n2 // tt2, b),
        in_specs=[pl.BlockSpec((1, 2, n1, tt2 * c), lambda j, bi: (bi, 0, 0, j)),
                  pl.BlockSpec((2 * n1, 2 * n1), lambda j, bi: (0, 0)),
                  pl.BlockSpec((1, n1, tt2), lambda j, bi: (j, 0, 0)),
                  pl.BlockSpec((1, n1, tt2), lambda j, bi: (j, 0, 0))],
        out_specs=pl.BlockSpec((1, 2, tt2, n1, c), lambda j, bi: (bi, 0, j, 0, 0)),
        out_shape=jax.ShapeDtypeStruct((b, 2, n2, n1, c), BF16),
        compiler_params=_cparams(2),
        name="dft_stage1",
    )(u, f1, twc, tws)
    c2, s2 = _cos_sin(n2)
    mat = jnp.asarray(np.concatenate([c2, s2], axis=1) * scale, F32).astype(BF16)
    y = _dft_real_call(mat, v.reshape(b, 2, n2, n1 * c), 8192)
    return y.reshape(b, n, c)


def _post_kernel(*refs, pieces):
    x_ref, mod_ref = refs[:2]
    mix, pos = [], 2
    for n_pieces in pieces:
        mix.append((refs[pos:pos + n_pieces], refs[pos + n_pieces]))
        pos += n_pieces + 1
    g1_ref, b1_ref, wg_ref, wu_ref, wd_ref, g2_ref, b2_ref, o_ref = refs[pos:]
    m = mod_ref[0]
    sub = _row_subtiles(x_ref.shape[1])
    x1s, hs, fs = [], [], []
    for r, rows in enumerate(sub):
        y = None
        for parts, w_ref in mix:
            a = parts[0][0, rows] if len(parts) == 1 else parts[r][0]
            t = _dot(a, w_ref[...])
            y = t if y is None else y + t
        x1 = _layer_norm(DN_ALPHA * x_ref[0, rows] + m[2:3] * y) * g1_ref[...] + b1_ref[...]
        x1s.append(x1)
        hs.append((_layer_norm(x1) * (1.0 + m[4:5]) + m[3:4]).astype(BF16))
    for h in hs:
        f = None
        lo = 0
        for width in FF_CHUNKS:
            sl = slice(lo, lo + width)
            lo += width
            act = _silu(_dot(h, wg_ref[:, sl])) * _dot(h, wu_ref[:, sl])
            t = _dot(act.astype(BF16), wd_ref[sl, :])
            f = t if f is None else f + t
        fs.append(f)
    for rows, x1, f in zip(sub, x1s, fs):
        o_ref[0, rows] = _layer_norm(DN_ALPHA * x1 + m[5:6] * f) * g2_ref[...] + b2_ref[...]


def _post_call(x, mod, mod_row, mixes, g1, b1, wg, wu, wd, g2, b2, tm):
    b, n, _ = x.shape
    const = lambda a: pl.BlockSpec(a.shape, lambda bi, i: (0,) * a.ndim, pipeline_mode=pl.Buffered(1))
    in_specs = [pl.BlockSpec((1, tm, D), lambda bi, i: (bi, i, 0)),
                pl.BlockSpec((1, 6, D), lambda bi, i: (mod_row(bi), 0, 0))]
    args = [x, mod]
    for parts, w in mixes:
        assert len(parts) in (1, len(_row_subtiles(tm)))
        for a in parts:
            in_specs.append(pl.BlockSpec((1, tm // len(parts), a.shape[2]), lambda bi, i: (bi, i, 0)))
        in_specs.append(const(w))
        args += [*parts, w]
    tail = [g1, b1, wg, wu, wd, g2, b2]
    in_specs += [const(a) for a in tail]
    return pl.pallas_call(
        functools.partial(_post_kernel, pieces=tuple(len(parts) for parts, _ in mixes)),
        grid=(b, n // tm),
        in_specs=in_specs,
        out_specs=pl.BlockSpec((1, tm, D), lambda bi, i: (bi, i, 0)),
        out_shape=jax.ShapeDtypeStruct((b, n, D), F32),
        compiler_params=_cparams(2),
        name="post",
    )(*args, *tail)


def _proj1_kernel(x_ref, mod_ref, rtab_ref, ctab_ref, mask_ref, w_ref, q_ref, k_ref, v_ref):
    m = mod_ref[0]
    sub = _row_subtiles(x_ref.shape[1])
    hs = [(_layer_norm(x_ref[0, rows]) * (1.0 + m[1:2]) + m[0:1]).astype(BF16) for rows in sub]
    us = [_dot(h, w_ref[...]) for h in hs]
    width = DIFF_HEADS * LANES
    for rows, u in zip(sub, us):
        cos, sa, sb = _rope_tile(rtab_ref, ctab_ref, mask_ref, rows)
        lane = lax.broadcasted_iota(jnp.int32, (u.shape[0], LANES), 1)
        first = lane < DIFF_HEAD_DIM
        ones_lane = (lane == 0).astype(BF16)
        for hd in range(DIFF_HEADS):
            sl = slice(hd * LANES, (hd + 1) * LANES)
            q = _rope(u[:, sl], cos, sa, sb, DIFF_HEAD_DIM // 4) * (DIFF_SCALE * LOG2E)
            q_ref[0, hd, 0, rows] = jnp.where(first, q, 0.0).astype(BF16)
            q_ref[0, hd, 1, rows] = jnp.where(first, 0.0, q).astype(BF16)
            ksl = slice(width + hd * LANES, width + (hd + 1) * LANES)
            k_ref[0, hd, rows] = _rope(u[:, ksl], cos, sa, sb, DIFF_HEAD_DIM // 4).astype(BF16)
            vsl = slice(2 * width + hd * LANES, 2 * width + (hd + 1) * LANES)
            v_ref[0, hd, rows, :LANES] = u[:, vsl].astype(BF16)
            v_ref[0, hd, rows, LANES:] = ones_lane


def _proj1_call(x, mod, mod_row, tables, w, tm):
    b, n, _ = x.shape
    head = pl.BlockSpec((1, DIFF_HEADS, tm, LANES), lambda bi, i: (bi, 0, i, 0))
    return pl.pallas_call(
        _proj1_kernel,
        grid=(b, n // tm),
        in_specs=[pl.BlockSpec((1, tm, D), lambda bi, i: (bi, i, 0)),
                  pl.BlockSpec((1, 6, D), lambda bi, i: (mod_row(bi), 0, 0)),
                  *_rope_specs(tm),
                  pl.BlockSpec(w.shape, lambda bi, i: (0, 0))],
        out_specs=[pl.BlockSpec((1, DIFF_HEADS, 2, tm, LANES), lambda bi, i: (bi, 0, 0, i, 0)), head,
                   pl.BlockSpec((1, DIFF_HEADS, tm, 2 * LANES), lambda bi, i: (bi, 0, i, 0))],
        out_shape=[jax.ShapeDtypeStruct((b, DIFF_HEADS, 2, n, LANES), BF16),
                   jax.ShapeDtypeStruct((b, DIFF_HEADS, n, LANES), BF16),
                   jax.ShapeDtypeStruct((b, DIFF_HEADS, n, 2 * LANES), BF16)],
        compiler_params=_cparams(2),
        name="proj1",
    )(x, mod, *tables, w)


def _pad_heads(w, heads, width):
    kdim = w.shape[0]
    w3 = jnp.pad(w.reshape(kdim, heads, width), ((0, 0), (0, 0), (0, LANES - width)))
    return w3.reshape(kdim, heads * LANES)


def _layer0_weights(w_in, q_norm, w_uq, kv_norm, w_ukv):
    o3 = FOURIER_WIDTH + MLA_Q_LORA + MLA_KV_LORA
    w_kr = jnp.zeros((D, LANES), F32).at[:, MLA_NOPE:MLA_NOPE + MLA_ROPE].set(w_in[:, o3:])
    wa = jnp.concatenate([w_in[:, :o3], w_kr], axis=1).astype(BF16)
    cc, sc = _cos_sin(FOURIER_GROUP_DIM)
    dft = jnp.asarray(np.concatenate([cc, -sc], axis=1), F32).astype(BF16)
    wuq = _pad_heads(w_uq, MLA_HEADS, MLA_NOPE + MLA_ROPE).astype(BF16)
    ukv = w_ukv.reshape(MLA_KV_LORA, MLA_HEADS, MLA_NOPE + MLA_V)
    wk = _pad_heads(ukv[:, :, :MLA_NOPE].reshape(MLA_KV_LORA, -1), MLA_HEADS, MLA_NOPE).astype(BF16)
    wv = _pad_heads(ukv[:, :, MLA_NOPE:].reshape(MLA_KV_LORA, -1), MLA_HEADS, MLA_V).astype(BF16)
    return wa, dft, q_norm.reshape(1, -1), wuq, kv_norm.reshape(1, -1), wk, wv


def kernel(x, c, ctx, c_ctx,
           l0_w_mod, l0_b_mod, l0_w_in, l0_q_norm, l0_w_uq, l0_kv_norm, l0_w_ukv, l0_w_out,
           l0_ln1_g, l0_ln1_b, l0_w_gate, l0_w_up, l0_w_down, l0_ln2_g, l0_ln2_b,
           l1_w_mod, l1_b_mod, l1_w_in, l1_lambda_q1, l1_lambda_k1, l1_lambda_q2, l1_lambda_k2,
           l1_subln, l1_w_out, l1_ln1_g, l1_ln1_b, l1_w_gate, l1_w_up, l1_w_down, l1_ln2_g, l1_ln2_b):
    b, n, _ = x.shape
    nc = ctx.shape[1]
    tm, tq = 512, 256
    assert b + 1 <= MOD_ROWS and n % max(tm, ATTN_TILES * tq) == 0 and nc % LANES == 0
    ctx_row = b
    lat_row = lambda bi: bi
    cx_row = lambda bi: ctx_row
    row = lambda a: a.reshape(1, -1)

    c_all = jnp.zeros((MOD_ROWS, D), F32).at[:b].set(c).at[b].set(c_ctx)
    mod0 = _mod_call(c_all, l0_w_mod, l0_b_mod)
    mod1 = _mod_call(c_all, l1_w_mod, l1_b_mod)
    ident = _identity_tables(nc)

    w0 = _layer0_weights(l0_w_in, l0_q_norm, l0_w_uq, l0_kv_norm, l0_w_ukv)
    tab0 = _rope_tables(n, MLA_ROPE, [MLA_NOPE])
    u, q, k, v = _proj0_call(x, mod0, lat_row, tab0, w0, tm, n2=n // DFT_N1)
    uc, qc, kc, vc = _proj0_call(ctx, mod0, cx_row, ident, w0, nc)
    att = _mla_attn_call(q, kc, vc, k, v, tq)
    attc = _mla_ctx_attn_call(qc, kc, vc)
    four = _fourier_positions(u, n)
    fourc = _fourier_positions_direct(uc)
    wo = l0_w_out.astype(BF16)
    ffn0 = (row(l0_ln1_g), row(l0_ln1_b), l0_w_gate.astype(BF16), l0_w_up.astype(BF16), l0_w_down.astype(BF16),
            row(l0_ln2_g), row(l0_ln2_b))
    x = _post_call(x, mod0, lat_row, [([four], wo[:FOURIER_WIDTH]), ([att], wo[FOURIER_WIDTH:])], *ffn0, tm)
    xc = _post_call(ctx, mod0, cx_row, [([fourc], wo[:FOURIER_WIDTH]), ([attc], wo[FOURIER_WIDTH:])], *ffn0, nc)

    lambda_init = 0.8 - 0.6 * math.exp(-0.3 * 1)
    w1 = l1_w_in.astype(BF16)
    tab1 = _rope_tables(n, DIFF_HEAD_DIM, [0, DIFF_HEAD_DIM])
    q, k, v = _proj1_call(x, mod1, lat_row, tab1, w1, tm)
    _, kc, vc = _proj1_call(xc, mod1, cx_row, ident, w1, nc)
    lam_params = jnp.stack([l1_lambda_q1, l1_lambda_k1, l1_lambda_q2, l1_lambda_k2])
    att = _diff_attn_call(lam_params, row(l1_subln), q, kc, vc, k, v, tq, lambda_init)
    ffn1 = (row(l1_ln1_g), row(l1_ln1_b), l1_w_gate.astype(BF16), l1_w_up.astype(BF16), l1_w_down.astype(BF16),
            row(l1_ln2_g), row(l1_ln2_b))
    return _post_call(x, mod1, lat_row, [([att], l1_w_out.astype(BF16))], *ffn1, tm)
```

```python
import functools
import math

import numpy as np
import jax
import jax.numpy as jnp
from jax import lax
from jax.experimental import pallas as pl
from jax.experimental.pallas import tpu as pltpu

F32 = jnp.float32
BF16 = jnp.bfloat16

D = 1024
DEPTH = 2
GRID_W = 64
ROPE_BASE = 10000.0
LN_EPS = 1e-6
RMS_EPS = 1e-6
DN_ALPHA = (2 * DEPTH) ** 0.25
LANES = 128
FOURIER_GROUPS = 4
FOURIER_GROUP_DIM = 128
FOURIER_WIDTH = 512
MLA_HEADS = 8
MLA_Q_LORA = 256
MLA_KV_LORA = 256
MLA_NOPE = 64
MLA_ROPE = 32
MLA_V = 64
MLA_SCALE = (MLA_NOPE + MLA_ROPE) ** -0.5
DIFF_HEADS = 8
DIFF_HEAD_DIM = 64
DIFF_SCALE = DIFF_HEAD_DIM ** -0.5
LOG2E = math.log2(math.e)
MXU_TILE = 256
FF_HIDDEN = 2816
FF_CHUNKS = (1536, 1280)
SUBTILE_ROWS = 256
MOD_ROWS = 8
DFT_N1 = 128
ATTN_TILES = 4
VMEM_LIMIT = 56 * 2 ** 20


def _cparams(n_axes):
    return pltpu.CompilerParams(dimension_semantics=("arbitrary",) * n_axes, vmem_limit_bytes=VMEM_LIMIT)


def _layer_norm(x):
    mu = jnp.mean(x, axis=-1, keepdims=True)
    xc = x - mu
    var = jnp.mean(xc * xc, axis=-1, keepdims=True)
    return xc * lax.rsqrt(var + LN_EPS)


def _rms(x):
    return x * lax.rsqrt(jnp.mean(x * x, axis=-1, keepdims=True) + RMS_EPS)


def _silu(x):
    return x * (1.0 / (1.0 + jnp.exp(-x)))


def _rope(x, cos, sa, sb, q):
    return x * cos + pltpu.roll(x, LANES - q, 1) * sa + pltpu.roll(x, q, 1) * sb


def _row_subtiles(tm):
    n_sub = max(1, tm // SUBTILE_ROWS)
    return [slice(r * (tm // n_sub), (r + 1) * (tm // n_sub)) for r in range(n_sub)]


def _dot(a, b):
    return jnp.dot(a, b, preferred_element_type=F32)


def _dot_nt(a, b):
    return lax.dot_general(a, b, (((1,), (1,)), ((), ())), preferred_element_type=F32)


def _mod_kernel(c_ref, w_ref, b_ref, o_ref):
    o_ref[...] = _dot(_silu(c_ref[...]), w_ref[...]) + b_ref[...]


def _mod_call(c_all, w, b):
    n_out = w.shape[1]
    tn = 1024
    return pl.pallas_call(
        _mod_kernel,
        grid=(n_out // tn,),
        in_specs=[pl.BlockSpec((MOD_ROWS, D), lambda j: (0, 0)),
                  pl.BlockSpec((D, tn), lambda j: (0, j)),
                  pl.BlockSpec((1, tn), lambda j: (0, j))],
        out_specs=pl.BlockSpec((MOD_ROWS, tn), lambda j: (0, j)),
        out_shape=jax.ShapeDtypeStruct((MOD_ROWS, n_out), F32),
        compiler_params=_cparams(1),
        name="adaln_mod",
    )(c_all, w, b.reshape(1, n_out)).reshape(MOD_ROWS, 6, D)


def _rope_tables(n, rope_dim, regions):
    half, quarter = rope_dim // 2, rope_dim // 4
    is_rope = np.zeros(LANES, bool)
    use_col = np.zeros(LANES, bool)
    is_x1 = np.zeros(LANES, bool)
    fi = np.zeros(LANES, np.int32)
    for off in regions:
        for r in range(rope_dim):
            j = off + r
            rr = r % half
            is_rope[j] = True
            use_col[j] = r >= half
            is_x1[j] = rr < quarter
            fi[j] = rr % quarter
    inv = (1.0 / (ROPE_BASE ** (jnp.arange(quarter, dtype=F32) / quarter)))[fi][None, :]
    ang_r = jnp.arange(n // GRID_W, dtype=jnp.int32).astype(F32)[:, None] * inv
    ang_c = jnp.arange(GRID_W, dtype=jnp.int32).astype(F32)[:, None] * inv

    def tables(ang):
        sin = jnp.sin(ang)
        return jnp.stack([jnp.where(is_rope[None, :], jnp.cos(ang), 1.0),
                          jnp.where((is_rope & is_x1)[None, :], -sin, 0.0),
                          jnp.where((is_rope & ~is_x1)[None, :], sin, 0.0)])

    return tables(ang_r), tables(ang_c), jnp.asarray(use_col[None, :], F32)


def _identity_tables(n):
    one = lambda rows: jnp.stack([jnp.ones((rows, LANES), F32), jnp.zeros((rows, LANES), F32),
                                  jnp.zeros((rows, LANES), F32)])
    return one(n // GRID_W), one(GRID_W), jnp.zeros((1, LANES), F32)


def _rope_specs(tm):
    return [pl.BlockSpec((3, tm // GRID_W, LANES), lambda bi, i: (0, i, 0)),
            pl.BlockSpec((3, GRID_W, LANES), lambda bi, i: (0, 0, 0)),
            pl.BlockSpec((1, LANES), lambda bi, i: (0, 0))]


def _rope_tile(rtab_ref, ctab_ref, mask_ref, rows):
    g0, g1 = rows.start // GRID_W, rows.stop // GRID_W
    by_col = mask_ref[...] > 0.5
    out = []
    for t in range(3):
        r = jnp.broadcast_to(rtab_ref[t, g0:g1][:, None, :], (g1 - g0, GRID_W, LANES))
        c = jnp.concatenate([ctab_ref[t]] * (g1 - g0), axis=0)
        out.append(jnp.where(by_col, c, r.reshape(rows.stop - rows.start, LANES)))
    return out


def _proj0_kernel(x_ref, mod_ref, rtab_ref, ctab_ref, mask_ref, wa_ref, dft_ref, qn_ref, wuq_ref, kvn_ref,
                  wk_ref, wv_ref, u_ref, q_ref, k_ref, v_ref):
    m = mod_ref[0]
    sub = _row_subtiles(x_ref.shape[1])
    hs = [(_layer_norm(x_ref[0, rows]) * (1.0 + m[1:2]) + m[0:1]).astype(BF16) for rows in sub]
    us = [_dot(h, wa_ref[...]) for h in hs]
    o1 = FOURIER_WIDTH
    o2 = o1 + MLA_Q_LORA
    o3 = o2 + MLA_KV_LORA
    for rows, u in zip(sub, us):
        for g in range(FOURIER_GROUPS):
            lo = g * FOURIER_GROUP_DIM
            z = _dot(u[:, lo:lo + FOURIER_GROUP_DIM].astype(BF16), dft_ref[...])
            u_ref[0, 0, rows, lo:lo + FOURIER_GROUP_DIM] = z[:, :FOURIER_GROUP_DIM].astype(BF16)
            u_ref[0, 1, rows, lo:lo + FOURIER_GROUP_DIM] = z[:, FOURIER_GROUP_DIM:].astype(BF16)
    cqs = [(_rms(u[:, o1:o2]) * qn_ref[...]).astype(BF16) for u in us]
    ckvs = [(_rms(u[:, o2:o3]) * kvn_ref[...]).astype(BF16) for u in us]
    qs = [_dot(cq, wuq_ref[...]) for cq in cqs]
    kns = [_dot(ckv, wk_ref[...]) for ckv in ckvs]
    vs = [_dot(ckv, wv_ref[...]) for ckv in ckvs]
    for rows, u, q, kn, v in zip(sub, us, qs, kns, vs):
        cos, sa, sb = _rope_tile(rtab_ref, ctab_ref, mask_ref, rows)
        kr = _rope(u[:, o3:o3 + LANES], cos, sa, sb, MLA_ROPE // 4)
        ones_lane = (lax.broadcasted_iota(jnp.int32, (u.shape[0], LANES), 1) == MLA_V).astype(F32)
        for hd in range(MLA_HEADS):
            sl = slice(hd * LANES, (hd + 1) * LANES)
            q_ref[0, hd, rows] = (_rope(q[:, sl], cos, sa, sb, MLA_ROPE // 4) * (MLA_SCALE * LOG2E)).astype(BF16)
            k_ref[0, hd, rows] = (kn[:, sl] + kr).astype(BF16)
            v_ref[0, hd, rows] = (v[:, sl] + ones_lane).astype(BF16)


def _proj0_call(x, mod, mod_row, tables, w, tm):
    b, n, _ = x.shape
    wa, dft, qn, wuq, kvn, wk, wv = w
    const = lambda shape: pl.BlockSpec(shape, lambda bi, i: (0,) * len(shape))
    head = pl.BlockSpec((1, MLA_HEADS, tm, LANES), lambda bi, i: (bi, 0, i, 0))
    return pl.pallas_call(
        _proj0_kernel,
        grid=(b, n // tm),
        in_specs=[pl.BlockSpec((1, tm, D), lambda bi, i: (bi, i, 0)),
                  pl.BlockSpec((1, 6, D), lambda bi, i: (mod_row(bi), 0, 0)),
                  *_rope_specs(tm),
                  const(wa.shape), const(dft.shape), const(qn.shape), const(wuq.shape), const(kvn.shape),
                  const(wk.shape), const(wv.shape)],
        out_specs=[pl.BlockSpec((1, 2, tm, FOURIER_WIDTH), lambda bi, i: (bi, 0, i, 0)), head, head, head],
        out_shape=[jax.ShapeDtypeStruct((b, 2, n, FOURIER_WIDTH), BF16)]
        + [jax.ShapeDtypeStruct((b, MLA_HEADS, n, LANES), BF16)] * 3,
        compiler_params=_cparams(2),
        name="proj0",
    )(x, mod, *tables, wa, dft, qn, wuq, kvn, wk, wv)


def _softmax_pv(q, kc, vc, k, v):
    s_c = _dot_nt(q, kc)
    m = jnp.max(s_c, axis=-1, keepdims=True)
    if k is not None:
        s_l = _dot_nt(q, k)
        m = jnp.maximum(m, jnp.max(s_l, axis=-1, keepdims=True))
    acc = _dot(jnp.exp2(s_c - m).astype(BF16), vc)
    if k is not None:
        acc = acc + _dot(jnp.exp2(s_l - m).astype(BF16), v)
    return acc


def _merge_head_pair(o_even, o_odd):
    lane = lax.broadcasted_iota(jnp.int32, o_even.shape, 1)
    return jnp.where(lane < MLA_V, o_even, pltpu.roll(o_odd, MLA_V, 1))


def _mla_ctx_attn_kernel(q_ref, kc_ref, vc_ref, o_ref):
    outs = []
    for hh in range(2):
        acc = _softmax_pv(q_ref[0, hh], kc_ref[0, hh], vc_ref[0, hh], None, None)
        outs.append(acc / acc[:, MLA_V:MLA_V + 1])
    o_ref[0] = _merge_head_pair(*outs).astype(BF16)


def _mla_ctx_attn_call(q, kc, vc):
    b, h, nc, _ = q.shape
    pair = pl.BlockSpec((1, 2, nc, LANES), lambda bi, hp: (bi, hp, 0, 0))
    return pl.pallas_call(
        _mla_ctx_attn_kernel,
        grid=(b, h // 2),
        in_specs=[pair, pair, pair],
        out_specs=pl.BlockSpec((1, nc, LANES), lambda bi, hp: (bi, 0, hp)),
        out_shape=jax.ShapeDtypeStruct((b, nc, h // 2 * LANES), BF16),
        compiler_params=_cparams(2),
        name="mla_ctx_attn",
    )(q, kc, vc)


def _scores_into(q, kc, k, s_ref, m_ref, slot):
    nc = kc.shape[0]
    s_c = _dot_nt(q, kc)
    s_l = _dot_nt(q, k)
    s_ref[slot, :, :nc] = s_c
    s_ref[slot, :, nc:] = s_l
    m_ref[slot] = jnp.maximum(jnp.max(s_c, axis=-1, keepdims=True), jnp.max(s_l, axis=-1, keepdims=True))


def _values_from(s_ref, m_ref, slot, vc, v):
    nc = vc.shape[0]
    m = m_ref[slot]
    return (_dot(jnp.exp2(s_ref[slot, :, :nc] - m).astype(BF16), vc)
            + _dot(jnp.exp2(s_ref[slot, :, nc:] - m).astype(BF16), v))


def _init_pipeline(scratch_refs):
    @pl.when(pl.program_id(0) == 0)
    def _():
        for r in scratch_refs:
            r[...] = jnp.zeros(r.shape, r.dtype)


def _mla_attn_kernel(q_ref, kc_ref, k_ref, vcp_ref, vp_ref, vce_ref, ve_ref, vco_ref, vo_ref, o_ref,
                     s_ref, m_ref, half_ref, stash_ref):
    _init_pipeline((s_ref, m_ref, half_ref, stash_ref))
    tq = half_ref.shape[0]
    n_tiles = o_ref.shape[1] // tq
    tile = lambda i: slice(i * tq, (i + 1) * tq)

    def head_out(slot, vc_ref, v_ref):
        acc = _values_from(s_ref, m_ref, slot, vc_ref[0, 0], v_ref[0, 0])
        return acc / acc[:, MLA_V:MLA_V + 1]

    _scores_into(q_ref[0, 0, tile(0)], kc_ref[0, 0], k_ref[0, 0], s_ref, m_ref, 0)
    o_ref[0, tile(n_tiles - 1)] = _merge_head_pair(half_ref[...], head_out(1, vcp_ref, vp_ref)).astype(BF16)
    o_ref[0, :(n_tiles - 1) * tq] = stash_ref[...]
    for i in range(n_tiles):
        _scores_into(q_ref[0, 1, tile(i)], kc_ref[0, 1], k_ref[0, 1], s_ref, m_ref, 1)
        even = head_out(0, vce_ref, ve_ref)
        if i + 1 < n_tiles:
            _scores_into(q_ref[0, 0, tile(i + 1)], kc_ref[0, 0], k_ref[0, 0], s_ref, m_ref, 0)
            stash_ref[tile(i)] = _merge_head_pair(even, head_out(1, vco_ref, vo_ref)).astype(BF16)
        else:
            half_ref[...] = even


def _tile_maps(n_b, n_h, n_q):
    last = n_b * n_h * n_q - 1

    def unravel(t):
        return t // (n_h * n_q), (t // n_q) % n_h, t % n_q

    cur = lambda t: unravel(jnp.minimum(t, last))
    prev = lambda t: unravel(jnp.maximum(t - 1, 0))
    return last + 2, cur, prev


def _mla_attn_call(q, kc, vc, k, v, tq):
    b, h, n, _ = q.shape
    nc = kc.shape[2]
    rows = ATTN_TILES * tq
    steps, cur, prev = _tile_maps(b, h // 2, n // rows)

    def pair(size, blk):
        def index(t):
            bi, hp, j = cur(t)
            return (bi, hp, j if blk else 0, 0)
        return pl.BlockSpec((1, 2, size, LANES), index)

    def one(size, which, parity):
        def index(t):
            bi, hp, _ = which(t)
            return (bi, 2 * hp + parity, 0, 0)
        return pl.BlockSpec((1, 1, size, LANES), index)

    def out_index(t):
        bi, hp, j = prev(t)
        return (bi, j, hp)

    return pl.pallas_call(
        _mla_attn_kernel,
        grid=(steps,),
        in_specs=[pair(rows, True), pair(nc, False), pair(n, False),
                  one(nc, prev, 1), one(n, prev, 1), one(nc, cur, 0), one(n, cur, 0), one(nc, cur, 1), one(n, cur, 1)],
        out_specs=pl.BlockSpec((1, rows, LANES), out_index),
        out_shape=jax.ShapeDtypeStruct((b, n, h // 2 * LANES), BF16),
        scratch_shapes=[pltpu.VMEM((2, tq, nc + n), F32), pltpu.VMEM((2, tq, 1), F32),
                        pltpu.VMEM((tq, LANES), F32), pltpu.VMEM((rows - tq, LANES), BF16)],
        compiler_params=_cparams(1),
        name="mla_attn",
    )(q, kc, k, vc, v, vc, v, vc, v)


def _diff_attn_kernel(lam_ref, sub_ref, q_ref, kc_ref, k_ref, vcp_ref, vp_ref, vcc_ref, vcur_ref, o_ref,
                      s_ref, m_ref, o0_ref, stash_ref, *, lambda_init):
    _init_pipeline((s_ref, m_ref, o0_ref, stash_ref))
    lp = lam_ref[...]
    lam = (jnp.exp(jnp.sum(lp[0:1] * lp[1:2], axis=-1, keepdims=True))
           - jnp.exp(jnp.sum(lp[2:3] * lp[3:4], axis=-1, keepdims=True)) + lambda_init)
    kc, k = kc_ref[0, 0], k_ref[0, 0]
    tq = o0_ref.shape[0]
    n_tiles = o_ref.shape[1] // tq
    tile = lambda i: slice(i * tq, (i + 1) * tq)

    def map_out(slot, vc_ref, v_ref):
        acc = _values_from(s_ref, m_ref, slot, vc_ref[0, 0], v_ref[0, 0])
        return acc[:, :LANES] / acc[:, LANES:LANES + 1]

    def finish(o0, o1):
        return (_rms(o0 - lam * o1) * sub_ref[...] * (1.0 - lambda_init)).astype(BF16)

    _scores_into(q_ref[0, 0, 0, tile(0)], kc, k, s_ref, m_ref, 0)
    o_ref[0, tile(n_tiles - 1)] = finish(o0_ref[...], map_out(1, vcp_ref, vp_ref))
    o_ref[0, :(n_tiles - 1) * tq] = stash_ref[...]
    for i in range(n_tiles):
        _scores_into(q_ref[0, 0, 1, tile(i)], kc, k, s_ref, m_ref, 1)
        o0 = map_out(0, vcc_ref, vcur_ref)
        if i + 1 < n_tiles:
            _scores_into(q_ref[0, 0, 0, tile(i + 1)], kc, k, s_ref, m_ref, 0)
            stash_ref[tile(i)] = finish(o0, map_out(1, vcc_ref, vcur_ref))
        else:
            o0_ref[...] = o0


def _diff_attn_call(lam_params, subln, q, kc, vc, k, v, tq, lambda_init):
    b, h, _, n, _ = q.shape
    nc = kc.shape[2]
    rows = ATTN_TILES * tq
    steps, cur, prev = _tile_maps(b, h, n // rows)

    def head(size, width, which):
        def index(t):
            bi, hd, _ = which(t)
            return (bi, hd, 0, 0)
        return pl.BlockSpec((1, 1, size, width), index)

    def q_index(t):
        bi, hd, j = cur(t)
        return (bi, hd, 0, j, 0)

    def out_index(t):
        bi, hd, j = prev(t)
        return (bi, j, hd)

    return pl.pallas_call(
        functools.partial(_diff_attn_kernel, lambda_init=lambda_init),
        grid=(steps,),
        in_specs=[pl.BlockSpec((4, DIFF_HEAD_DIM), lambda t: (0, 0)),
                  pl.BlockSpec((1, LANES), lambda t: (0, 0)),
                  pl.BlockSpec((1, 1, 2, rows, LANES), q_index),
                  head(nc, LANES, cur), head(n, LANES, cur),
                  head(nc, 2 * LANES, prev), head(n, 2 * LANES, prev),
                  head(nc, 2 * LANES, cur), head(n, 2 * LANES, cur)],
        out_specs=pl.BlockSpec((1, rows, LANES), out_index),
        out_shape=jax.ShapeDtypeStruct((b, n, h * LANES), BF16),
        scratch_shapes=[pltpu.VMEM((2, tq, nc + n), F32), pltpu.VMEM((2, tq, 1), F32),
                        pltpu.VMEM((tq, LANES), F32), pltpu.VMEM((rows - tq, LANES), BF16)],
        compiler_params=_cparams(1),
        name="diff_attn",
    )(lam_params, subln, q, kc, k, vc, v, vc, v)


def _dft_stage1_kernel(u_ref, f_ref, twc_ref, tws_ref, v_ref, *, tt2):
    f = f_ref[...]
    r = _dot(f[:, :DFT_N1], u_ref[0, 0]) + _dot(f[:, DFT_N1:], u_ref[0, 1])
    vr, vi = r[:DFT_N1], r[DFT_N1:]
    twc, tws = twc_ref[0], tws_ref[0]
    for jj in range(tt2):
        c, s = twc[:, jj:jj + 1], tws[:, jj:jj + 1]
        sl = slice(jj * FOURIER_WIDTH, (jj + 1) * FOURIER_WIDTH)
        v_ref[0, 0, jj] = (vr[:, sl] * c + vi[:, sl] * s).astype(BF16)
        v_ref[0, 1, jj] = (vi[:, sl] * c - vr[:, sl] * s).astype(BF16)


def _dft_real_kernel(m_ref, v_ref, o_ref, *, kdim):
    mat = m_ref[...]
    o_ref[0] = (_dot(mat[:, :kdim], v_ref[0, 0]) + _dot(mat[:, kdim:], v_ref[0, 1])).astype(BF16)


def _dft_real_call(mat, v, tc):
    b, _, kdim, c = v.shape
    mrows = mat.shape[0]
    return pl.pallas_call(
        functools.partial(_dft_real_kernel, kdim=kdim),
        grid=(b, c // tc),
        in_specs=[pl.BlockSpec(mat.shape, lambda bi, j: (0, 0)),
                  pl.BlockSpec((1, 2, kdim, tc), lambda bi, j: (bi, 0, 0, j))],
        out_specs=pl.BlockSpec((1, mrows, tc), lambda bi, j: (bi, 0, j)),
        out_shape=jax.ShapeDtypeStruct((b, mrows, c), BF16),
        compiler_params=_cparams(2),
        name="dft_real",
    )(mat, v)


def _cos_sin(n):
    idx = np.arange(n, dtype=np.int64)
    ang = 2.0 * np.pi * ((idx[:, None] * idx[None, :]) % n) / n
    return np.cos(ang), np.sin(ang)


def _fourier_positions(u):
    b, _, n, c = u.shape
    scale = 1.0 / math.sqrt(n * FOURIER_GROUP_DIM)
    if n <= 256:
        cm, sm = _cos_sin(n)
        mat = jnp.asarray(np.concatenate([cm, sm], axis=1) * scale, F32).astype(BF16)
        return _dft_real_call(mat, u, c)
    n1, n2 = DFT_N1, n // DFT_N1
    c1, s1 = _cos_sin(n1)
    f1 = jnp.asarray(np.block([[c1, s1], [-s1, c1]]), F32).astype(BF16)
    k1 = np.arange(n1, dtype=np.int64)[:, None]
    t2 = np.arange(n2, dtype=np.int64)[None, :]
    ang = 2.0 * np.pi * ((k1 * t2) % n) / n
    tt2 = 4
    tw_shape = lambda a: jnp.asarray(a.reshape(n1, n2 // tt2, tt2).transpose(1, 0, 2), F32)
    twc, tws = tw_shape(np.cos(ang)), tw_shape(np.sin(ang))
    v = pl.pallas_call(
        functools.partial(_dft_stage1_kernel, tt2=tt2),
        grid=(n2 // tt2, b),
        in_specs=[pl.BlockSpec((1, 2, n1, tt2 * c), lambda j, bi: (bi, 0, 0, j)),
                  pl.BlockSpec((2 * n1, 2 * n1), lambda j, bi: (0, 0)),
                  pl.BlockSpec((1, n1, tt2), lambda j, bi: (j, 0, 0)),
                  pl.BlockSpec((1, n1, tt2), lambda j, bi: (j, 0, 0))],
        out_specs=pl.BlockSpec((1, 2, tt2, n1, c), lambda j, bi: (bi, 0, j, 0, 0)),
        out_shape=jax.ShapeDtypeStruct((b, 2, n2, n1, c), BF16),
        compiler_params=_cparams(2),
        name="dft_stage1",
    )(u.reshape(b, 2, n1, n2 * c), f1, twc, tws)
    c2, s2 = _cos_sin(n2)
    mat = jnp.asarray(np.concatenate([c2, s2], axis=1) * scale, F32).astype(BF16)
    y = _dft_real_call(mat, v.reshape(b, 2, n2, n1 * c), 8192)
    return y.reshape(b, n, c)


def _post_kernel(*refs, pieces):
    x_ref, mod_ref = refs[:2]
    mix, pos = [], 2
    for n_pieces in pieces:
        mix.append((refs[pos:pos + n_pieces], refs[pos + n_pieces]))
        pos += n_pieces + 1
    g1_ref, b1_ref, wg_ref, wu_ref, wd_ref, g2_ref, b2_ref, o_ref = refs[pos:]
    m = mod_ref[0]
    sub = _row_subtiles(x_ref.shape[1])
    x1s, hs, fs = [], [], []
    for r, rows in enumerate(sub):
        y = None
        for parts, w_ref in mix:
            a = parts[0][0, rows] if len(parts) == 1 else parts[r][0]
            t = _dot(a, w_ref[...])
            y = t if y is None else y + t
        x1 = _layer_norm(DN_ALPHA * x_ref[0, rows] + m[2:3] * y) * g1_ref[...] + b1_ref[...]
        x1s.append(x1)
        hs.append((_layer_norm(x1) * (1.0 + m[4:5]) + m[3:4]).astype(BF16))
    for h in hs:
        f = None
        lo = 0
        for width in FF_CHUNKS:
            sl = slice(lo, lo + width)
            lo += width
            act = _silu(_dot(h, wg_ref[:, sl])) * _dot(h, wu_ref[:, sl])
            t = _dot(act.astype(BF16), wd_ref[sl, :])
            f = t if f is None else f + t
        fs.append(f)
    for rows, x1, f in zip(sub, x1s, fs):
        o_ref[0, rows] = _layer_norm(DN_ALPHA * x1 + m[5:6] * f) * g2_ref[...] + b2_ref[...]


def _post_call(x, mod, mod_row, mixes, g1, b1, wg, wu, wd, g2, b2, tm):
    b, n, _ = x.shape
    const = lambda a: pl.BlockSpec(a.shape, lambda bi, i: (0,) * a.ndim, pipeline_mode=pl.Buffered(1))
    in_specs = [pl.BlockSpec((1, tm, D), lambda bi, i: (bi, i, 0)),
                pl.BlockSpec((1, 6, D), lambda bi, i: (mod_row(bi), 0, 0))]
    args = [x, mod]
    for parts, w in mixes:
        assert len(parts) in (1, len(_row_subtiles(tm)))
        for a in parts:
            in_specs.append(pl.BlockSpec((1, tm // len(parts), a.shape[2]), lambda bi, i: (bi, i, 0)))
        in_specs.append(const(w))
        args += [*parts, w]
    tail = [g1, b1, wg, wu, wd, g2, b2]
    in_specs += [const(a) for a in tail]
    return pl.pallas_call(
        functools.partial(_post_kernel, pieces=tuple(len(parts) for parts, _ in mixes)),
        grid=(b, n // tm),
        in_specs=in_specs,
        out_specs=pl.BlockSpec((1, tm, D), lambda bi, i: (bi, i, 0)),
        out_shape=jax.ShapeDtypeStruct((b, n, D), F32),
        compiler_params=_cparams(2),
        name="post",
    )(*args, *tail)


def _proj1_kernel(x_ref, mod_ref, rtab_ref, ctab_ref, mask_ref, w_ref, q_ref, k_ref, v_ref):
    m = mod_ref[0]
    sub = _row_subtiles(x_ref.shape[1])
    hs = [(_layer_norm(x_ref[0, rows]) * (1.0 + m[1:2]) + m[0:1]).astype(BF16) for rows in sub]
    us = [_dot(h, w_ref[...]) for h in hs]
    width = DIFF_HEADS * LANES
    for rows, u in zip(sub, us):
        cos, sa, sb = _rope_tile(rtab_ref, ctab_ref, mask_ref, rows)
        lane = lax.broadcasted_iota(jnp.int32, (u.shape[0], LANES), 1)
        first = lane < DIFF_HEAD_DIM
        ones_lane = (lane == 0).astype(BF16)
        for hd in range(DIFF_HEADS):
            sl = slice(hd * LANES, (hd + 1) * LANES)
            q = _rope(u[:, sl], cos, sa, sb, DIFF_HEAD_DIM // 4) * (DIFF_SCALE * LOG2E)
            q_ref[0, hd, 0, rows] = jnp.where(first, q, 0.0).astype(BF16)
            q_ref[0, hd, 1, rows] = jnp.where(first, 0.0, q).astype(BF16)
            ksl = slice(width + hd * LANES, width + (hd + 1) * LANES)
            k_ref[0, hd, rows] = _rope(u[:, ksl], cos, sa, sb, DIFF_HEAD_DIM // 4).astype(BF16)
            vsl = slice(2 * width + hd * LANES, 2 * width + (hd + 1) * LANES)
            v_ref[0, hd, rows, :LANES] = u[:, vsl].astype(BF16)
            v_ref[0, hd, rows, LANES:] = ones_lane


def _proj1_call(x, mod, mod_row, tables, w, tm):
    b, n, _ = x.shape
    head = pl.BlockSpec((1, DIFF_HEADS, tm, LANES), lambda bi, i: (bi, 0, i, 0))
    return pl.pallas_call(
        _proj1_kernel,
        grid=(b, n // tm),
        in_specs=[pl.BlockSpec((1, tm, D), lambda bi, i: (bi, i, 0)),
                  pl.BlockSpec((1, 6, D), lambda bi, i: (mod_row(bi), 0, 0)),
                  *_rope_specs(tm),
                  pl.BlockSpec(w.shape, lambda bi, i: (0, 0))],
        out_specs=[pl.BlockSpec((1, DIFF_HEADS, 2, tm, LANES), lambda bi, i: (bi, 0, 0, i, 0)), head,
                   pl.BlockSpec((1, DIFF_HEADS, tm, 2 * LANES), lambda bi, i: (bi, 0, i, 0))],
        out_shape=[jax.ShapeDtypeStruct((b, DIFF_HEADS, 2, n, LANES), BF16),
                   jax.ShapeDtypeStruct((b, DIFF_HEADS, n, LANES), BF16),
                   jax.ShapeDtypeStruct((b, DIFF_HEADS, n, 2 * LANES), BF16)],
        compiler_params=_cparams(2),
        name="proj1",
    )(x, mod, *tables, w)


def _pad_heads(w, heads, width):
    kdim = w.shape[0]
    w3 = jnp.pad(w.reshape(kdim, heads, width), ((0, 0), (0, 0), (0, LANES - width)))
    return w3.reshape(kdim, heads * LANES)


def _layer0_weights(w_in, q_norm, w_uq, kv_norm, w_ukv):
    o3 = FOURIER_WIDTH + MLA_Q_LORA + MLA_KV_LORA
    w_kr = jnp.zeros((D, LANES), F32).at[:, MLA_NOPE:MLA_NOPE + MLA_ROPE].set(w_in[:, o3:])
    wa = jnp.concatenate([w_in[:, :o3], w_kr], axis=1).astype(BF16)
    cc, sc = _cos_sin(FOURIER_GROUP_DIM)
    dft = jnp.asarray(np.concatenate([cc, -sc], axis=1), F32).astype(BF16)
    wuq = _pad_heads(w_uq, MLA_HEADS, MLA_NOPE + MLA_ROPE).astype(BF16)
    ukv = w_ukv.reshape(MLA_KV_LORA, MLA_HEADS, MLA_NOPE + MLA_V)
    wk = _pad_heads(ukv[:, :, :MLA_NOPE].reshape(MLA_KV_LORA, -1), MLA_HEADS, MLA_NOPE).astype(BF16)
    wv = _pad_heads(ukv[:, :, MLA_NOPE:].reshape(MLA_KV_LORA, -1), MLA_HEADS, MLA_V).astype(BF16)
    return wa, dft, q_norm.reshape(1, -1), wuq, kv_norm.reshape(1, -1), wk, wv


def kernel(x, c, ctx, c_ctx,
           l0_w_mod, l0_b_mod, l0_w_in, l0_q_norm, l0_w_uq, l0_kv_norm, l0_w_ukv, l0_w_out,
           l0_ln1_g, l0_ln1_b, l0_w_gate, l0_w_up, l0_w_down, l0_ln2_g, l0_ln2_b,
           l1_w_mod, l1_b_mod, l1_w_in, l1_lambda_q1, l1_lambda_k1, l1_lambda_q2, l1_lambda_k2,
           l1_subln, l1_w_out, l1_ln1_g, l1_ln1_b, l1_w_gate, l1_w_up, l1_w_down, l1_ln2_g, l1_ln2_b):
    b, n, _ = x.shape
    nc = ctx.shape[1]
    tm, tm_post, tq = 512, 1024, 256
    assert b + 1 <= MOD_ROWS and n % max(tm_post, ATTN_TILES * tq) == 0 and nc % LANES == 0
    ctx_row = b
    lat_row = lambda bi: bi
    cx_row = lambda bi: ctx_row
    row = lambda a: a.reshape(1, -1)

    c_all = jnp.zeros((MOD_ROWS, D), F32).at[:b].set(c).at[b].set(c_ctx)
    mod0 = _mod_call(c_all, l0_w_mod, l0_b_mod)
    mod1 = _mod_call(c_all, l1_w_mod, l1_b_mod)
    ident = _identity_tables(nc)

    w0 = _layer0_weights(l0_w_in, l0_q_norm, l0_w_uq, l0_kv_norm, l0_w_ukv)
    tab0 = _rope_tables(n, MLA_ROPE, [MLA_NOPE])
    u, q, k, v = _proj0_call(x, mod0, lat_row, tab0, w0, tm)
    uc, qc, kc, vc = _proj0_call(ctx, mod0, cx_row, ident, w0, nc)
    att = _mla_attn_call(q, kc, vc, k, v, tq)
    attc = _mla_ctx_attn_call(qc, kc, vc)
    four = _fourier_positions(u)
    fourc = _fourier_positions(uc)
    wo = l0_w_out.astype(BF16)
    ffn0 = (row(l0_ln1_g), row(l0_ln1_b), l0_w_gate.astype(BF16), l0_w_up.astype(BF16), l0_w_down.astype(BF16),
            row(l0_ln2_g), row(l0_ln2_b))
    x = _post_call(x, mod0, lat_row, [([four], wo[:FOURIER_WIDTH]), ([att], wo[FOURIER_WIDTH:])], *ffn0, tm_post)
    xc = _post_call(ctx, mod0, cx_row, [([fourc], wo[:FOURIER_WIDTH]), ([attc], wo[FOURIER_WIDTH:])], *ffn0, nc)

    lambda_init = 0.8 - 0.6 * math.exp(-0.3 * 1)
    w1 = l1_w_in.astype(BF16)
    tab1 = _rope_tables(n, DIFF_HEAD_DIM, [0, DIFF_HEAD_DIM])
    q, k, v = _proj1_call(x, mod1, lat_row, tab1, w1, tm)
    _, kc, vc = _proj1_call(xc, mod1, cx_row, ident, w1, nc)
    lam_params = jnp.stack([l1_lambda_q1, l1_lambda_k1, l1_lambda_q2, l1_lambda_k2])
    att = _diff_attn_call(lam_params, row(l1_subln), q, kc, vc, k, v, tq, lambda_init)
    ffn1 = (row(l1_ln1_g), row(l1_ln1_b), l1_w_gate.astype(BF16), l1_w_up.astype(BF16), l1_w_down.astype(BF16),
            row(l1_ln2_g), row(l1_ln2_b))
    return _post_call(x, mod1, lat_row, [([att], l1_w_out.astype(BF16))], *ffn1, tm_post)
```

```python
import functools
import math

import numpy as np
import jax
import jax.numpy as jnp
from jax import lax
from jax.experimental import pallas as pl
from jax.experimental.pallas import tpu as pltpu

F32 = jnp.float32
BF16 = jnp.bfloat16

D = 1024
DEPTH = 2
GRID_W = 64
ROPE_BASE = 10000.0
LN_EPS = 1e-6
RMS_EPS = 1e-6
DN_ALPHA = (2 * DEPTH) ** 0.25
LANES = 128
FOURIER_GROUPS = 4
FOURIER_GROUP_DIM = 128
FOURIER_WIDTH = 512
MLA_HEADS = 8
MLA_Q_LORA = 256
MLA_KV_LORA = 256
MLA_NOPE = 64
MLA_ROPE = 32
MLA_V = 64
MLA_SCALE = (MLA_NOPE + MLA_ROPE) ** -0.5
DIFF_HEADS = 8
DIFF_HEAD_DIM = 64
DIFF_SCALE = DIFF_HEAD_DIM ** -0.5
LOG2E = math.log2(math.e)
MXU_TILE = 256
FF_HIDDEN = 2816
FF_CHUNKS = (1536, 1280)
SUBTILE_ROWS = 256
MOD_ROWS = 8
DFT_N1 = 128
ATTN_TILES = 4
VMEM_LIMIT = 56 * 2 ** 20


def _cparams(n_axes):
    return pltpu.CompilerParams(dimension_semantics=("arbitrary",) * n_axes, vmem_limit_bytes=VMEM_LIMIT)


def _layer_norm(x):
    mu = jnp.mean(x, axis=-1, keepdims=True)
    xc = x - mu
    var = jnp.mean(xc * xc, axis=-1, keepdims=True)
    return xc * lax.rsqrt(var + LN_EPS)


def _rms(x):
    return x * lax.rsqrt(jnp.mean(x * x, axis=-1, keepdims=True) + RMS_EPS)


def _silu(x):
    return x * (1.0 / (1.0 + jnp.exp(-x)))


def _rope(x, cos, sa, sb, q):
    return x * cos + pltpu.roll(x, LANES - q, 1) * sa + pltpu.roll(x, q, 1) * sb


def _row_subtiles(tm):
    n_sub = max(1, tm // SUBTILE_ROWS)
    return [slice(r * (tm // n_sub), (r + 1) * (tm // n_sub)) for r in range(n_sub)]


def _dot(a, b):
    return jnp.dot(a, b, preferred_element_type=F32)


def _dot_nt(a, b):
    return lax.dot_general(a, b, (((1,), (1,)), ((), ())), preferred_element_type=F32)


def _mod_kernel(c_ref, w_ref, b_ref, o_ref):
    o_ref[...] = _dot(_silu(c_ref[...]), w_ref[...]) + b_ref[...]


def _mod_call(c_all, w, b):
    n_out = w.shape[1]
    tn = 1024
    return pl.pallas_call(
        _mod_kernel,
        grid=(n_out // tn,),
        in_specs=[pl.BlockSpec((MOD_ROWS, D), lambda j: (0, 0)),
                  pl.BlockSpec((D, tn), lambda j: (0, j)),
                  pl.BlockSpec((1, tn), lambda j: (0, j))],
        out_specs=pl.BlockSpec((MOD_ROWS, tn), lambda j: (0, j)),
        out_shape=jax.ShapeDtypeStruct((MOD_ROWS, n_out), F32),
        compiler_params=_cparams(1),
        name="adaln_mod",
    )(c_all, w, b.reshape(1, n_out)).reshape(MOD_ROWS, 6, D)


def _rope_tables(n, rope_dim, regions):
    half, quarter = rope_dim // 2, rope_dim // 4
    is_rope = np.zeros(LANES, bool)
    use_col = np.zeros(LANES, bool)
    is_x1 = np.zeros(LANES, bool)
    fi = np.zeros(LANES, np.int32)
    for off in regions:
        for r in range(rope_dim):
            j = off + r
            rr = r % half
            is_rope[j] = True
            use_col[j] = r >= half
            is_x1[j] = rr < quarter
            fi[j] = rr % quarter
    inv = (1.0 / (ROPE_BASE ** (jnp.arange(quarter, dtype=F32) / quarter)))[fi][None, :]
    ang_r = jnp.arange(n // GRID_W, dtype=jnp.int32).astype(F32)[:, None] * inv
    ang_c = jnp.arange(GRID_W, dtype=jnp.int32).astype(F32)[:, None] * inv

    def tables(ang):
        sin = jnp.sin(ang)
        return jnp.stack([jnp.where(is_rope[None, :], jnp.cos(ang), 1.0),
                          jnp.where((is_rope & is_x1)[None, :], -sin, 0.0),
                          jnp.where((is_rope & ~is_x1)[None, :], sin, 0.0)])

    return tables(ang_r), tables(ang_c), jnp.asarray(use_col[None, :], F32)


def _identity_tables(n):
    one = lambda rows: jnp.stack([jnp.ones((rows, LANES), F32), jnp.zeros((rows, LANES), F32),
                                  jnp.zeros((rows, LANES), F32)])
    return one(n // GRID_W), one(GRID_W), jnp.zeros((1, LANES), F32)


def _rope_specs(tm):
    return [pl.BlockSpec((3, tm // GRID_W, LANES), lambda bi, i: (0, i, 0)),
            pl.BlockSpec((3, GRID_W, LANES), lambda bi, i: (0, 0, 0)),
            pl.BlockSpec((1, LANES), lambda bi, i: (0, 0))]


def _rope_tile(rtab_ref, ctab_ref, mask_ref, rows):
    g0, g1 = rows.start // GRID_W, rows.stop // GRID_W
    by_col = mask_ref[...] > 0.5
    out = []
    for t in range(3):
        r = jnp.broadcast_to(rtab_ref[t, g0:g1][:, None, :], (g1 - g0, GRID_W, LANES))
        c = jnp.concatenate([ctab_ref[t]] * (g1 - g0), axis=0)
        out.append(jnp.where(by_col, c, r.reshape(rows.stop - rows.start, LANES)))
    return out


def _proj0_kernel(x_ref, mod_ref, rtab_ref, ctab_ref, mask_ref, wa_ref, dft_ref, qn_ref, wuq_ref, kvn_ref,
                  wk_ref, wv_ref, u_ref, q_ref, k_ref, v_ref):
    m = mod_ref[0]
    sub = _row_subtiles(x_ref.shape[1])
    hs = [(_layer_norm(x_ref[0, rows]) * (1.0 + m[1:2]) + m[0:1]).astype(BF16) for rows in sub]
    us = [_dot(h, wa_ref[...]) for h in hs]
    o1 = FOURIER_WIDTH
    o2 = o1 + MLA_Q_LORA
    o3 = o2 + MLA_KV_LORA
    for rows, u in zip(sub, us):
        for g in range(FOURIER_GROUPS):
            lo = g * FOURIER_GROUP_DIM
            z = _dot(u[:, lo:lo + FOURIER_GROUP_DIM].astype(BF16), dft_ref[...])
            u_ref[0, 0, rows, lo:lo + FOURIER_GROUP_DIM] = z[:, :FOURIER_GROUP_DIM].astype(BF16)
            u_ref[0, 1, rows, lo:lo + FOURIER_GROUP_DIM] = z[:, FOURIER_GROUP_DIM:].astype(BF16)
    cqs = [(_rms(u[:, o1:o2]) * qn_ref[...]).astype(BF16) for u in us]
    ckvs = [(_rms(u[:, o2:o3]) * kvn_ref[...]).astype(BF16) for u in us]
    qs = [_dot(cq, wuq_ref[...]) for cq in cqs]
    kns = [_dot(ckv, wk_ref[...]) for ckv in ckvs]
    vs = [_dot(ckv, wv_ref[...]) for ckv in ckvs]
    for rows, u, q, kn, v in zip(sub, us, qs, kns, vs):
        cos, sa, sb = _rope_tile(rtab_ref, ctab_ref, mask_ref, rows)
        kr = _rope(u[:, o3:o3 + LANES], cos, sa, sb, MLA_ROPE // 4)
        ones_lane = (lax.broadcasted_iota(jnp.int32, (u.shape[0], LANES), 1) == MLA_V).astype(F32)
        for hd in range(MLA_HEADS):
            sl = slice(hd * LANES, (hd + 1) * LANES)
            q_ref[0, hd, rows] = (_rope(q[:, sl], cos, sa, sb, MLA_ROPE // 4) * (MLA_SCALE * LOG2E)).astype(BF16)
            k_ref[0, hd, rows] = (kn[:, sl] + kr).astype(BF16)
            v_ref[0, hd, rows] = (v[:, sl] + ones_lane).astype(BF16)


def _proj0_call(x, mod, mod_row, tables, w, tm):
    b, n, _ = x.shape
    wa, dft, qn, wuq, kvn, wk, wv = w
    const = lambda shape: pl.BlockSpec(shape, lambda bi, i: (0,) * len(shape))
    head = pl.BlockSpec((1, MLA_HEADS, tm, LANES), lambda bi, i: (bi, 0, i, 0))
    return pl.pallas_call(
        _proj0_kernel,
        grid=(b, n // tm),
        in_specs=[pl.BlockSpec((1, tm, D), lambda bi, i: (bi, i, 0)),
                  pl.BlockSpec((1, 6, D), lambda bi, i: (mod_row(bi), 0, 0)),
                  *_rope_specs(tm),
                  const(wa.shape), const(dft.shape), const(qn.shape), const(wuq.shape), const(kvn.shape),
                  const(wk.shape), const(wv.shape)],
        out_specs=[pl.BlockSpec((1, 2, tm, FOURIER_WIDTH), lambda bi, i: (bi, 0, i, 0)), head, head, head],
        out_shape=[jax.ShapeDtypeStruct((b, 2, n, FOURIER_WIDTH), BF16)]
        + [jax.ShapeDtypeStruct((b, MLA_HEADS, n, LANES), BF16)] * 3,
        compiler_params=_cparams(2),
        name="proj0",
    )(x, mod, *tables, wa, dft, qn, wuq, kvn, wk, wv)


def _softmax_pv(q, kc, vc, k, v):
    s_c = _dot_nt(q, kc)
    m = jnp.max(s_c, axis=-1, keepdims=True)
    if k is not None:
        s_l = _dot_nt(q, k)
        m = jnp.maximum(m, jnp.max(s_l, axis=-1, keepdims=True))
    acc = _dot(jnp.exp2(s_c - m).astype(BF16), vc)
    if k is not None:
        acc = acc + _dot(jnp.exp2(s_l - m).astype(BF16), v)
    return acc


def _merge_head_pair(o_even, o_odd):
    lane = lax.broadcasted_iota(jnp.int32, o_even.shape, 1)
    return jnp.where(lane < MLA_V, o_even, pltpu.roll(o_odd, MLA_V, 1))


def _mla_ctx_attn_kernel(q_ref, kc_ref, vc_ref, o_ref):
    outs = []
    for hh in range(2):
        acc = _softmax_pv(q_ref[0, hh], kc_ref[0, hh], vc_ref[0, hh], None, None)
        outs.append(acc / acc[:, MLA_V:MLA_V + 1])
    o_ref[0] = _merge_head_pair(*outs).astype(BF16)


def _mla_ctx_attn_call(q, kc, vc):
    b, h, nc, _ = q.shape
    pair = pl.BlockSpec((1, 2, nc, LANES), lambda bi, hp: (bi, hp, 0, 0))
    return pl.pallas_call(
        _mla_ctx_attn_kernel,
        grid=(b, h // 2),
        in_specs=[pair, pair, pair],
        out_specs=pl.BlockSpec((1, nc, LANES), lambda bi, hp: (bi, 0, hp)),
        out_shape=jax.ShapeDtypeStruct((b, nc, h // 2 * LANES), BF16),
        compiler_params=_cparams(2),
        name="mla_ctx_attn",
    )(q, kc, vc)


def _scores_into(q, kc, k, s_ref, m_ref, slot):
    nc = kc.shape[0]
    s_c = _dot_nt(q, kc)
    s_l = _dot_nt(q, k)
    s_ref[slot, :, :nc] = s_c
    s_ref[slot, :, nc:] = s_l
    m_ref[slot] = jnp.maximum(jnp.max(s_c, axis=-1, keepdims=True), jnp.max(s_l, axis=-1, keepdims=True))


def _values_from(s_ref, m_ref, slot, vc, v):
    nc = vc.shape[0]
    m = m_ref[slot]
    return (_dot(jnp.exp2(s_ref[slot, :, :nc] - m).astype(BF16), vc)
            + _dot(jnp.exp2(s_ref[slot, :, nc:] - m).astype(BF16), v))


def _init_pipeline(scratch_refs):
    @pl.when(pl.program_id(0) == 0)
    def _():
        for r in scratch_refs:
            r[...] = jnp.zeros(r.shape, r.dtype)


def _mla_attn_kernel(q_ref, kc_ref, k_ref, vcp_ref, vp_ref, vce_ref, ve_ref, vco_ref, vo_ref, o_ref,
                     s_ref, m_ref, half_ref, stash_ref):
    _init_pipeline((s_ref, m_ref, half_ref, stash_ref))
    tq = half_ref.shape[0]
    n_tiles = o_ref.shape[1] // tq
    tile = lambda i: slice(i * tq, (i + 1) * tq)

    def head_out(slot, vc_ref, v_ref):
        acc = _values_from(s_ref, m_ref, slot, vc_ref[0, 0], v_ref[0, 0])
        return acc / acc[:, MLA_V:MLA_V + 1]

    _scores_into(q_ref[0, 0, tile(0)], kc_ref[0, 0], k_ref[0, 0], s_ref, m_ref, 0)
    o_ref[0, tile(n_tiles - 1)] = _merge_head_pair(half_ref[...], head_out(1, vcp_ref, vp_ref)).astype(BF16)
    o_ref[0, :(n_tiles - 1) * tq] = stash_ref[...]
    for i in range(n_tiles):
        _scores_into(q_ref[0, 1, tile(i)], kc_ref[0, 1], k_ref[0, 1], s_ref, m_ref, 1)
        even = head_out(0, vce_ref, ve_ref)
        if i + 1 < n_tiles:
            _scores_into(q_ref[0, 0, tile(i + 1)], kc_ref[0, 0], k_ref[0, 0], s_ref, m_ref, 0)
            stash_ref[tile(i)] = _merge_head_pair(even, head_out(1, vco_ref, vo_ref)).astype(BF16)
        else:
            half_ref[...] = even


def _tile_maps(n_b, n_h, n_q):
    last = n_b * n_h * n_q - 1

    def unravel(t):
        return t // (n_h * n_q), (t // n_q) % n_h, t % n_q

    cur = lambda t: unravel(jnp.minimum(t, last))
    prev = lambda t: unravel(jnp.maximum(t - 1, 0))
    return last + 2, cur, prev


def _mla_attn_call(q, kc, vc, k, v, tq):
    b, h, n, _ = q.shape
    nc = kc.shape[2]
    rows = ATTN_TILES * tq
    steps, cur, prev = _tile_maps(b, h // 2, n // rows)

    def pair(size, blk):
        def index(t):
            bi, hp, j = cur(t)
            return (bi, hp, j if blk else 0, 0)
        return pl.BlockSpec((1, 2, size, LANES), index)

    def one(size, which, parity):
        def index(t):
            bi, hp, _ = which(t)
            return (bi, 2 * hp + parity, 0, 0)
        return pl.BlockSpec((1, 1, size, LANES), index)

    def out_index(t):
        bi, hp, j = prev(t)
        return (bi, j, hp)

    return pl.pallas_call(
        _mla_attn_kernel,
        grid=(steps,),
        in_specs=[pair(rows, True), pair(nc, False), pair(n, False),
                  one(nc, prev, 1), one(n, prev, 1), one(nc, cur, 0), one(n, cur, 0), one(nc, cur, 1), one(n, cur, 1)],
        out_specs=pl.BlockSpec((1, rows, LANES), out_index),
        out_shape=jax.ShapeDtypeStruct((b, n, h // 2 * LANES), BF16),
        scratch_shapes=[pltpu.VMEM((2, tq, nc + n), F32), pltpu.VMEM((2, tq, 1), F32),
                        pltpu.VMEM((tq, LANES), F32), pltpu.VMEM((rows - tq, LANES), BF16)],
        compiler_params=_cparams(1),
        name="mla_attn",
    )(q, kc, k, vc, v, vc, v, vc, v)


def _diff_attn_kernel(lam_ref, sub_ref, q_ref, kc_ref, k_ref, vcp_ref, vp_ref, vcc_ref, vcur_ref, o_ref,
                      s_ref, m_ref, o0_ref, stash_ref, *, lambda_init):
    _init_pipeline((s_ref, m_ref, o0_ref, stash_ref))
    lp = lam_ref[...]
    lam = (jnp.exp(jnp.sum(lp[0:1] * lp[1:2], axis=-1, keepdims=True))
           - jnp.exp(jnp.sum(lp[2:3] * lp[3:4], axis=-1, keepdims=True)) + lambda_init)
    kc, k = kc_ref[0, 0], k_ref[0, 0]
    tq = o0_ref.shape[0]
    n_tiles = o_ref.shape[1] // tq
    tile = lambda i: slice(i * tq, (i + 1) * tq)

    def map_out(slot, vc_ref, v_ref):
        acc = _values_from(s_ref, m_ref, slot, vc_ref[0, 0], v_ref[0, 0])
        return acc[:, :LANES] / acc[:, LANES:LANES + 1]

    def finish(o0, o1):
        return (_rms(o0 - lam * o1) * sub_ref[...] * (1.0 - lambda_init)).astype(BF16)

    _scores_into(q_ref[0, 0, 0, tile(0)], kc, k, s_ref, m_ref, 0)
    o_ref[0, tile(n_tiles - 1)] = finish(o0_ref[...], map_out(1, vcp_ref, vp_ref))
    o_ref[0, :(n_tiles - 1) * tq] = stash_ref[...]
    for i in range(n_tiles):
        _scores_into(q_ref[0, 0, 1, tile(i)], kc, k, s_ref, m_ref, 1)
        o0 = map_out(0, vcc_ref, vcur_ref)
        if i + 1 < n_tiles:
            _scores_into(q_ref[0, 0, 0, tile(i + 1)], kc, k, s_ref, m_ref, 0)
            stash_ref[tile(i)] = finish(o0, map_out(1, vcc_ref, vcur_ref))
        else:
            o0_ref[...] = o0


def _diff_attn_call(lam_params, subln, q, kc, vc, k, v, tq, lambda_init):
    b, h, _, n, _ = q.shape
    nc = kc.shape[2]
    rows = ATTN_TILES * tq
    steps, cur, prev = _tile_maps(b, h, n // rows)

    def head(size, width, which):
        def index(t):
            bi, hd, _ = which(t)
            return (bi, hd, 0, 0)
        return pl.BlockSpec((1, 1, size, width), index)

    def q_index(t):
        bi, hd, j = cur(t)
        return (bi, hd, 0, j, 0)

    def out_index(t):
        bi, hd, j = prev(t)
        return (bi, j, hd)

    return pl.pallas_call(
        functools.partial(_diff_attn_kernel, lambda_init=lambda_init),
        grid=(steps,),
        in_specs=[pl.BlockSpec((4, DIFF_HEAD_DIM), lambda t: (0, 0)),
                  pl.BlockSpec((1, LANES), lambda t: (0, 0)),
                  pl.BlockSpec((1, 1, 2, rows, LANES), q_index),
                  head(nc, LANES, cur), head(n, LANES, cur),
                  head(nc, 2 * LANES, prev), head(n, 2 * LANES, prev),
                  head(nc, 2 * LANES, cur), head(n, 2 * LANES, cur)],
        out_specs=pl.BlockSpec((1, rows, LANES), out_index),
        out_shape=jax.ShapeDtypeStruct((b, n, h * LANES), BF16),
        scratch_shapes=[pltpu.VMEM((2, tq, nc + n), F32), pltpu.VMEM((2, tq, 1), F32),
                        pltpu.VMEM((tq, LANES), F32), pltpu.VMEM((rows - tq, LANES), BF16)],
        compiler_params=_cparams(1),
        name="diff_attn",
    )(lam_params, subln, q, kc, k, vc, v, vc, v)


def _dft_stage1_kernel(u_ref, f_ref, twc_ref, tws_ref, v_ref, *, tt2):
    f = f_ref[...]
    r = _dot(f[:, :DFT_N1], u_ref[0, 0]) + _dot(f[:, DFT_N1:], u_ref[0, 1])
    vr, vi = r[:DFT_N1], r[DFT_N1:]
    twc, tws = twc_ref[0], tws_ref[0]
    for jj in range(tt2):
        c, s = twc[:, jj:jj + 1], tws[:, jj:jj + 1]
        sl = slice(jj * FOURIER_WIDTH, (jj + 1) * FOURIER_WIDTH)
        v_ref[0, 0, jj] = (vr[:, sl] * c + vi[:, sl] * s).astype(BF16)
        v_ref[0, 1, jj] = (vi[:, sl] * c - vr[:, sl] * s).astype(BF16)


def _dft_real_kernel(m_ref, v_ref, o_ref, *, kdim):
    mat = m_ref[...]
    o_ref[0] = (_dot(mat[:, :kdim], v_ref[0, 0]) + _dot(mat[:, kdim:], v_ref[0, 1])).astype(BF16)


def _dft_real_call(mat, v, tc):
    b, _, kdim, c = v.shape
    mrows = mat.shape[0]
    return pl.pallas_call(
        functools.partial(_dft_real_kernel, kdim=kdim),
        grid=(b, c // tc),
        in_specs=[pl.BlockSpec(mat.shape, lambda bi, j: (0, 0)),
                  pl.BlockSpec((1, 2, kdim, tc), lambda bi, j: (bi, 0, 0, j))],
        out_specs=pl.BlockSpec((1, mrows, tc), lambda bi, j: (bi, 0, j)),
        out_shape=jax.ShapeDtypeStruct((b, mrows, c), BF16),
        compiler_params=_cparams(2),
        name="dft_real",
    )(mat, v)


def _cos_sin(n):
    idx = np.arange(n, dtype=np.int64)
    ang = 2.0 * np.pi * ((idx[:, None] * idx[None, :]) % n) / n
    return np.cos(ang), np.sin(ang)


def _fourier_positions(u):
    b, _, n, c = u.shape
    scale = 1.0 / math.sqrt(n * FOURIER_GROUP_DIM)
    if n <= 256:
        cm, sm = _cos_sin(n)
        mat = jnp.asarray(np.concatenate([cm, sm], axis=1) * scale, F32).astype(BF16)
        return _dft_real_call(mat, u, c)
    n1, n2 = DFT_N1, n // DFT_N1
    c1, s1 = _cos_sin(n1)
    f1 = jnp.asarray(np.block([[c1, s1], [-s1, c1]]), F32).astype(BF16)
    k1 = np.arange(n1, dtype=np.int64)[:, None]
    t2 = np.arange(n2, dtype=np.int64)[None, :]
    ang = 2.0 * np.pi * ((k1 * t2) % n) / n
    tt2 = 8
    tw_shape = lambda a: jnp.asarray(a.reshape(n1, n2 // tt2, tt2).transpose(1, 0, 2), F32)
    twc, tws = tw_shape(np.cos(ang)), tw_shape(np.sin(ang))
    v = pl.pallas_call(
        functools.partial(_dft_stage1_kernel, tt2=tt2),
        grid=(n2 // tt2, b),
        in_specs=[pl.BlockSpec((1, 2, n1, tt2 * c), lambda j, bi: (bi, 0, 0, j)),
                  pl.BlockSpec((2 * n1, 2 * n1), lambda j, bi: (0, 0)),
                  pl.BlockSpec((1, n1, tt2), lambda j, bi: (j, 0, 0)),
                  pl.BlockSpec((1, n1, tt2), lambda j, bi: (j, 0, 0))],
        out_specs=pl.BlockSpec((1, 2, tt2, n1, c), lambda j, bi: (bi, 0, j, 0, 0)),
        out_shape=jax.ShapeDtypeStruct((b, 2, n2, n1, c), BF16),
        compiler_params=_cparams(2),
        name="dft_stage1",
    )(u.reshape(b, 2, n1, n2 * c), f1, twc, tws)
    c2, s2 = _cos_sin(n2)
    mat = jnp.asarray(np.concatenate([c2, s2], axis=1) * scale, F32).astype(BF16)
    y = _dft_real_call(mat, v.reshape(b, 2, n2, n1 * c), 16384)
    return y.reshape(b, n, c)


def _post_kernel(*refs, pieces):
    x_ref, mod_ref = refs[:2]
    mix, pos = [], 2
    for n_pieces in pieces:
        mix.append((refs[pos:pos + n_pieces], refs[pos + n_pieces]))
        pos += n_pieces + 1
    g1_ref, b1_ref, wg_ref, wu_ref, wd_ref, g2_ref, b2_ref, o_ref = refs[pos:]
    m = mod_ref[0]
    sub = _row_subtiles(x_ref.shape[1])
    x1s, hs, fs = [], [], []
    for r, rows in enumerate(sub):
        y = None
        for parts, w_ref in mix:
            a = parts[0][0, rows] if len(parts) == 1 else parts[r][0]
            t = _dot(a, w_ref[...])
            y = t if y is None else y + t
        x1 = _layer_norm(DN_ALPHA * x_ref[0, rows] + m[2:3] * y) * g1_ref[...] + b1_ref[...]
        x1s.append(x1)
        hs.append((_layer_norm(x1) * (1.0 + m[4:5]) + m[3:4]).astype(BF16))
    for h in hs:
        f = None
        lo = 0
        for width in FF_CHUNKS:
            sl = slice(lo, lo + width)
            lo += width
            act = _silu(_dot(h, wg_ref[:, sl])) * _dot(h, wu_ref[:, sl])
            t = _dot(act.astype(BF16), wd_ref[sl, :])
            f = t if f is None else f + t
        fs.append(f)
    for rows, x1, f in zip(sub, x1s, fs):
        o_ref[0, rows] = _layer_norm(DN_ALPHA * x1 + m[5:6] * f) * g2_ref[...] + b2_ref[...]


def _post_call(x, mod, mod_row, mixes, g1, b1, wg, wu, wd, g2, b2, tm):
    b, n, _ = x.shape
    const = lambda a: pl.BlockSpec(a.shape, lambda bi, i: (0,) * a.ndim, pipeline_mode=pl.Buffered(1))
    in_specs = [pl.BlockSpec((1, tm, D), lambda bi, i: (bi, i, 0)),
                pl.BlockSpec((1, 6, D), lambda bi, i: (mod_row(bi), 0, 0))]
    args = [x, mod]
    for parts, w in mixes:
        assert len(parts) in (1, len(_row_subtiles(tm)))
        for a in parts:
            in_specs.append(pl.BlockSpec((1, tm // len(parts), a.shape[2]), lambda bi, i: (bi, i, 0)))
        in_specs.append(const(w))
        args += [*parts, w]
    tail = [g1, b1, wg, wu, wd, g2, b2]
    in_specs += [const(a) for a in tail]
    return pl.pallas_call(
        functools.partial(_post_kernel, pieces=tuple(len(parts) for parts, _ in mixes)),
        grid=(b, n // tm),
        in_specs=in_specs,
        out_specs=pl.BlockSpec((1, tm, D), lambda bi, i: (bi, i, 0)),
        out_shape=jax.ShapeDtypeStruct((b, n, D), F32),
        compiler_params=_cparams(2),
        name="post",
    )(*args, *tail)


def _proj1_kernel(x_ref, mod_ref, rtab_ref, ctab_ref, mask_ref, w_ref, q_ref, k_ref, v_ref):
    m = mod_ref[0]
    sub = _row_subtiles(x_ref.shape[1])
    hs = [(_layer_norm(x_ref[0, rows]) * (1.0 + m[1:2]) + m[0:1]).astype(BF16) for rows in sub]
    us = [_dot(h, w_ref[...]) for h in hs]
    width = DIFF_HEADS * LANES
    for rows, u in zip(sub, us):
        cos, sa, sb = _rope_tile(rtab_ref, ctab_ref, mask_ref, rows)
        lane = lax.broadcasted_iota(jnp.int32, (u.shape[0], LANES), 1)
        first = lane < DIFF_HEAD_DIM
        ones_lane = (lane == 0).astype(BF16)
        for hd in range(DIFF_HEADS):
            sl = slice(hd * LANES, (hd + 1) * LANES)
            q = _rope(u[:, sl], cos, sa, sb, DIFF_HEAD_DIM // 4) * (DIFF_SCALE * LOG2E)
            q_ref[0, hd, 0, rows] = jnp.where(first, q, 0.0).astype(BF16)
            q_ref[0, hd, 1, rows] = jnp.where(first, 0.0, q).astype(BF16)
            ksl = slice(width + hd * LANES, width + (hd + 1) * LANES)
            k_ref[0, hd, rows] = _rope(u[:, ksl], cos, sa, sb, DIFF_HEAD_DIM // 4).astype(BF16)
            vsl = slice(2 * width + hd * LANES, 2 * width + (hd + 1) * LANES)
            v_ref[0, hd, rows, :LANES] = u[:, vsl].astype(BF16)
            v_ref[0, hd, rows, LANES:] = ones_lane


def _proj1_call(x, mod, mod_row, tables, w, tm):
    b, n, _ = x.shape
    head = pl.BlockSpec((1, DIFF_HEADS, tm, LANES), lambda bi, i: (bi, 0, i, 0))
    return pl.pallas_call(
        _proj1_kernel,
        grid=(b, n // tm),
        in_specs=[pl.BlockSpec((1, tm, D), lambda bi, i: (bi, i, 0)),
                  pl.BlockSpec((1, 6, D), lambda bi, i: (mod_row(bi), 0, 0)),
                  *_rope_specs(tm),
                  pl.BlockSpec(w.shape, lambda bi, i: (0, 0))],
        out_specs=[pl.BlockSpec((1, DIFF_HEADS, 2, tm, LANES), lambda bi, i: (bi, 0, 0, i, 0)), head,
                   pl.BlockSpec((1, DIFF_HEADS, tm, 2 * LANES), lambda bi, i: (bi, 0, i, 0))],
        out_shape=[jax.ShapeDtypeStruct((b, DIFF_HEADS, 2, n, LANES), BF16),
                   jax.ShapeDtypeStruct((b, DIFF_HEADS, n, LANES), BF16),
                   jax.ShapeDtypeStruct((b, DIFF_HEADS, n, 2 * LANES), BF16)],
        compiler_params=_cparams(2),
        name="proj1",
    )(x, mod, *tables, w)


def _pad_heads(w, heads, width):
    kdim = w.shape[0]
    w3 = jnp.pad(w.reshape(kdim, heads, width), ((0, 0), (0, 0), (0, LANES - width)))
    return w3.reshape(kdim, heads * LANES)


def _layer0_weights(w_in, q_norm, w_uq, kv_norm, w_ukv):
    o3 = FOURIER_WIDTH + MLA_Q_LORA + MLA_KV_LORA
    w_kr = jnp.zeros((D, LANES), F32).at[:, MLA_NOPE:MLA_NOPE + MLA_ROPE].set(w_in[:, o3:])
    wa = jnp.concatenate([w_in[:, :o3], w_kr], axis=1).astype(BF16)
    cc, sc = _cos_sin(FOURIER_GROUP_DIM)
    dft = jnp.asarray(np.concatenate([cc, -sc], axis=1), F32).astype(BF16)
    wuq = _pad_heads(w_uq, MLA_HEADS, MLA_NOPE + MLA_ROPE).astype(BF16)
    ukv = w_ukv.reshape(MLA_KV_LORA, MLA_HEADS, MLA_NOPE + MLA_V)
    wk = _pad_heads(ukv[:, :, :MLA_NOPE].reshape(MLA_KV_LORA, -1), MLA_HEADS, MLA_NOPE).astype(BF16)
    wv = _pad_heads(ukv[:, :, MLA_NOPE:].reshape(MLA_KV_LORA, -1), MLA_HEADS, MLA_V).astype(BF16)
    return wa, dft, q_norm.reshape(1, -1), wuq, kv_norm.reshape(1, -1), wk, wv


def kernel(x, c, ctx, c_ctx,
           l0_w_mod, l0_b_mod, l0_w_in, l0_q_norm, l0_w_uq, l0_kv_norm, l0_w_ukv, l0_w_out,
           l0_ln1_g, l0_ln1_b, l0_w_gate, l0_w_up, l0_w_down, l0_ln2_g, l0_ln2_b,
           l1_w_mod, l1_b_mod, l1_w_in, l1_lambda_q1, l1_lambda_k1, l1_lambda_q2, l1_lambda_k2,
           l1_subln, l1_w_out, l1_ln1_g, l1_ln1_b, l1_w_gate, l1_w_up, l1_w_down, l1_ln2_g, l1_ln2_b):
    b, n, _ = x.shape
    nc = ctx.shape[1]
    tm, tm_post, tq = 512, 1024, 256
    assert b + 1 <= MOD_ROWS and n % max(tm_post, ATTN_TILES * tq) == 0 and nc % LANES == 0
    ctx_row = b
    lat_row = lambda bi: bi
    cx_row = lambda bi: ctx_row
    row = lambda a: a.reshape(1, -1)

    c_all = jnp.zeros((MOD_ROWS, D), F32).at[:b].set(c).at[b].set(c_ctx)
    mod0 = _mod_call(c_all, l0_w_mod, l0_b_mod)
    mod1 = _mod_call(c_all, l1_w_mod, l1_b_mod)
    ident = _identity_tables(nc)

    w0 = _layer0_weights(l0_w_in, l0_q_norm, l0_w_uq, l0_kv_norm, l0_w_ukv)
    tab0 = _rope_tables(n, MLA_ROPE, [MLA_NOPE])
    u, q, k, v = _proj0_call(x, mod0, lat_row, tab0, w0, tm)
    uc, qc, kc, vc = _proj0_call(ctx, mod0, cx_row, ident, w0, nc)
    att = _mla_attn_call(q, kc, vc, k, v, tq)
    attc = _mla_ctx_attn_call(qc, kc, vc)
    four = _fourier_positions(u)
    fourc = _fourier_positions(uc)
    wo = l0_w_out.astype(BF16)
    ffn0 = (row(l0_ln1_g), row(l0_ln1_b), l0_w_gate.astype(BF16), l0_w_up.astype(BF16), l0_w_down.astype(BF16),
            row(l0_ln2_g), row(l0_ln2_b))
    x = _post_call(x, mod0, lat_row, [([four], wo[:FOURIER_WIDTH]), ([att], wo[FOURIER_WIDTH:])], *ffn0, tm_post)
    xc = _post_call(ctx, mod0, cx_row, [([fourc], wo[:FOURIER_WIDTH]), ([attc], wo[FOURIER_WIDTH:])], *ffn0, nc)

    lambda_init = 0.8 - 0.6 * math.exp(-0.3 * 1)
    w1 = l1_w_in.astype(BF16)
    tab1 = _rope_tables(n, DIFF_HEAD_DIM, [0, DIFF_HEAD_DIM])
    q, k, v = _proj1_call(x, mod1, lat_row, tab1, w1, tm)
    _, kc, vc = _proj1_call(xc, mod1, cx_row, ident, w1, nc)
    lam_params = jnp.stack([l1_lambda_q1, l1_lambda_k1, l1_lambda_q2, l1_lambda_k2])
    att = _diff_attn_call(lam_params, row(l1_subln), q, kc, vc, k, v, tq, lambda_init)
    ffn1 = (row(l1_ln1_g), row(l1_ln1_b), l1_w_gate.astype(BF16), l1_w_up.astype(BF16), l1_w_down.astype(BF16),
            row(l1_ln2_g), row(l1_ln2_b))
    return _post_call(x, mod1, lat_row, [([att], l1_w_out.astype(BF16))], *ffn1, tm_post)
```
